```python
import jax, jax.numpy as jnp
from jax import lax
import numpy as np

D_MODEL = 1024
BATCH = 8
SEQ = 2048
DEPTH = 2
DEC_BATCH = 128
DEC_SEQ = 8
PAST_LEN = 16384
PAGE_SIZE = 128

D_CONV = D_MODEL
CONV_WIDTH = 31
CONV_BUF = CONV_WIDTH - 1
D_RWKV = D_MODEL
HEAD_DIM = 64
N_HEADS = D_RWKV // HEAD_DIM
LORA_W = 64
LORA_A = 64
LORA_G = 128
C_PLAIN = 2 * D_CONV + 2 * D_MODEL
C_RWKV = 3 * D_RWKV + LORA_W + LORA_A + LORA_G
C_IN = C_PLAIN + C_RWKV
N_GROUPS = 4
EXPERTS_PER_GROUP = 8
N_EXPERTS = N_GROUPS * EXPERTS_PER_GROUP
TOP_K = 2
D_EXPERT = 256
RMS_EPS = 1e-6
LN_EPS = 1e-5
GN_EPS = 64e-5

kernel_name = "conv_rwkv7_gated_hmoe_adaln_step"


def _rmsnorm(x, g):
    xf = x.astype(jnp.float32)
    y = xf * lax.rsqrt(jnp.mean(xf * xf, axis=-1, keepdims=True) + RMS_EPS)
    return (y * g.astype(jnp.float32)).astype(x.dtype)


def _layernorm(x, g, b, eps):
    xf = x.astype(jnp.float32)
    mu = jnp.mean(xf, axis=-1, keepdims=True)
    var = jnp.mean(jnp.square(xf - mu), axis=-1, keepdims=True)
    y = (xf - mu) * lax.rsqrt(var + eps)
    return (y * g.astype(jnp.float32) + b.astype(jnp.float32)).astype(x.dtype)


def _conv_branch(conv_val, conv_gate, conv_buf, conv_w, conv_b, ln_g, ln_b):
    u = conv_val * jax.nn.sigmoid(conv_gate)
    full = jnp.concatenate([conv_buf.astype(u.dtype), u], axis=1)
    z = lax.conv_general_dilated(full, conv_w[:, None, :].astype(u.dtype), window_strides=(1,), padding="VALID",
                                 dimension_numbers=("NWC", "WIO", "NWC"), feature_group_count=D_CONV)
    z = z + conv_b.astype(u.dtype)
    z = _layernorm(z, ln_g, ln_b, LN_EPS)
    return jax.nn.silu(z), full[:, -CONV_BUF:]


def _wkv_scan(r, w, k, v, kk, a, s0):
    def step(S, inp):
        r_t, w_t, k_t, v_t, kk_t, a_t = inp
        sa = jnp.einsum("bhij,bhj->bhi", S, kk_t)
        S = S * w_t[:, :, None, :] - sa[..., None] * (kk_t * a_t)[:, :, None, :] + v_t[..., None] * k_t[:, :, None, :]
        return S, jnp.einsum("bhij,bhj->bhi", S, r_t)
    xs = tuple(jnp.moveaxis(t, 1, 0) for t in (r, w, k, v, kk, a))
    s_new, out = lax.scan(step, s0, xs)
    return jnp.moveaxis(out, 0, 1), s_new


def _rwkv_branch(r, k, v, xw, xa, xg, s0, w0, w_up, a0, a_up, g_up, k_k, k_a, r_k, lnx_g, lnx_b):
    bsz, t = r.shape[0], r.shape[1]
    f = lambda z: z.astype(jnp.float32)
    r, k, v, xw, xa, xg = (f(z) for z in (r, k, v, xw, xa, xg))
    decay = jnp.exp(-jnp.exp(-jax.nn.softplus(-(f(w0) + jnp.tanh(xw) @ f(w_up))) - 0.5))
    a = jax.nn.sigmoid(f(a0) + xa @ f(a_up))
    g = jax.nn.sigmoid(xg) @ f(g_up)
    heads = lambda z: z.reshape(bsz, t, N_HEADS, HEAD_DIM)
    kk = heads(k * f(k_k))
    kk = kk * lax.rsqrt(jnp.sum(kk * kk, axis=-1, keepdims=True) + 1e-12)
    k = k * (1.0 + (a - 1.0) * f(k_a))
    rh, kh, vh = heads(r), heads(k), heads(v)
    out, s_new = _wkv_scan(rh, heads(decay), kh, vh, kk, heads(a), f(s0))
    out = _layernorm(out, lnx_g.reshape(N_HEADS, HEAD_DIM), lnx_b.reshape(N_HEADS, HEAD_DIM), GN_EPS)
    bonus = jnp.sum(rh * kh * f(r_k), axis=-1, keepdims=True) * vh
    y = (out + bonus).reshape(bsz, t, D_RWKV) * g
    return y, s_new


def _mixer(h, h_prev, conv_buf, s0, P, l):
    hx = jnp.concatenate([h_prev[:, None, :].astype(h.dtype), h], axis=1)
    proj = hx @ P["w_in"][l]
    cur = proj[:, 1:]
    rw = cur[..., C_PLAIN:]
    rw = rw + (proj[:, :-1, C_PLAIN:] - rw) * P["mu_shift"][l]
    conv_val, conv_gate, gate_a, gate_b = jnp.split(
        cur[..., :C_PLAIN], [D_CONV, 2 * D_CONV, 2 * D_CONV + D_MODEL], axis=-1)
    r, k, v, xw, xa, xg = jnp.split(
        rw, [D_RWKV, 2 * D_RWKV, 3 * D_RWKV, 3 * D_RWKV + LORA_W, 3 * D_RWKV + LORA_W + LORA_A], axis=-1)
    y_a, new_buf = _conv_branch(conv_val, conv_gate, conv_buf, P["conv_w"][l], P["conv_b"][l],
                                P["conv_ln_g"][l], P["conv_ln_b"][l])
    y_b, s_new = _rwkv_branch(r, k, v, xw, xa, xg, s0, P["rw_w0"][l], P["rw_w_up"][l], P["rw_a0"][l],
                              P["rw_a_up"][l], P["rw_g_up"][l], P["rw_k_k"][l], P["rw_k_a"][l],
                              P["rw_r_k"][l], P["rw_lnx_g"][l], P["rw_lnx_b"][l])
    merged = jax.nn.sigmoid(gate_a) * y_a + jax.nn.sigmoid(gate_b) * y_b.astype(h.dtype)
    return merged @ P["w_out"][l], new_buf, h[:, -1], s_new


def _moe(h, P, l):
    gl = (h @ P["moe_w_group"][l]).astype(jnp.float32) + P["moe_b_group"][l].astype(jnp.float32)
    gp = jax.nn.softmax(gl, axis=-1)
    g_onehot = jax.nn.one_hot(jnp.argmax(gl, axis=-1), N_GROUPS, dtype=jnp.float32)
    pg = jnp.sum(gp * g_onehot, axis=-1, keepdims=True)
    el = (h @ P["moe_w_expert"][l]).astype(jnp.float32) + P["moe_b_expert"][l].astype(jnp.float32)
    el = el.reshape(h.shape[0], h.shape[1], N_GROUPS, EXPERTS_PER_GROUP)
    el_sel = jnp.einsum("btge,btg->bte", el, g_onehot)
    top_v, top_i = lax.top_k(el_sel, TOP_K)
    wts = jax.nn.softmax(top_v, axis=-1) * pg
    grp = jnp.argmax(g_onehot, axis=-1)
    eid = grp[..., None] * EXPERTS_PER_GROUP + top_i
    combine = jnp.einsum("btk,btke->bte", wts, jax.nn.one_hot(eid, N_EXPERTS, dtype=jnp.float32)).astype(h.dtype)
    hg = jnp.einsum("btd,edf->btef", h, P["moe_w_gate"][l])
    hu = jnp.einsum("btd,edf->btef", h, P["moe_w_up"][l])
    act = jax.nn.silu(hg) * hu * combine[..., None]
    return jnp.einsum("btef,efd->btd", act, P["moe_w_down"][l])


def _trunk(x, c, shift0, conv0, wkv0, P):
    bufs, shifts, states = [], [], []
    for l in range(DEPTH):
        mod = jax.nn.silu(c) @ P["ada_w"][l] + P["ada_b"][l]
        sh1, sc1, gt1, sh2, sc2, gt2 = (m[:, None, :] for m in jnp.split(mod, 6, axis=-1))
        h = _rmsnorm(x, P["norm_mix_g"][l]) * (1.0 + sc1) + sh1
        y, nb, ns, nw = _mixer(h, shift0[l], conv0[l], wkv0[l], P, l)
        x = x + gt1 * y
        h = _rmsnorm(x, P["norm_ffn_g"][l]) * (1.0 + sc2) + sh2
        x = x + gt2 * _moe(h, P, l)
        bufs.append(nb)
        shifts.append(ns)
        states.append(nw)
    return _rmsnorm(x, P["final_norm_g"]), jnp.stack(bufs), jnp.stack(shifts), jnp.stack(states)


def setup_inputs(seed: int = 0) -> dict:
    key = jax.random.key(seed)
    ks = iter(jax.random.split(key, 48))
    def nrm(shape, scale):
        return scale * jax.random.normal(next(ks), shape, jnp.float32)
    def near_one(shape):
        return 1.0 + nrm(shape, 0.05)
    return {
        "x_prompt": nrm((BATCH, SEQ, D_MODEL), 1.0),
        "x_sample": nrm((DEC_BATCH, DEC_SEQ, D_MODEL), 1.0),
        "state_conv": nrm((DEPTH, DEC_BATCH, CONV_BUF, D_CONV), 0.5),
        "state_shift": nrm((DEPTH, DEC_BATCH, D_MODEL), 1.0),
        "state_wkv": nrm((DEPTH, DEC_BATCH, N_HEADS, HEAD_DIM, HEAD_DIM), 0.3),
        "c_prompt": nrm((BATCH, D_MODEL), 1.0),
        "c_sample": nrm((DEC_BATCH, D_MODEL), 1.0),
        "norm_mix_g": near_one((DEPTH, D_MODEL)),
        "norm_ffn_g": near_one((DEPTH, D_MODEL)),
        "final_norm_g": near_one((D_MODEL,)),
        "ada_w": nrm((DEPTH, D_MODEL, 6 * D_MODEL), 0.5 * D_MODEL ** -0.5),
        "ada_b": nrm((DEPTH, 6 * D_MODEL), 0.02),
        "w_in": nrm((DEPTH, D_MODEL, C_IN), D_MODEL ** -0.5),
        "mu_shift": jax.random.uniform(next(ks), (DEPTH, C_RWKV), jnp.float32),
        "conv_w": nrm((DEPTH, CONV_WIDTH, D_CONV), CONV_WIDTH ** -0.5),
        "conv_b": nrm((DEPTH, D_CONV), 0.02),
        "conv_ln_g": near_one((DEPTH, D_CONV)),
        "conv_ln_b": nrm((DEPTH, D_CONV), 0.02),
        "rw_w0": -2.0 + nrm((DEPTH, D_RWKV), 0.5),
        "rw_w_up": nrm((DEPTH, LORA_W, D_RWKV), 0.5 * LORA_W ** -0.5),
        "rw_a0": nrm((DEPTH, D_RWKV), 0.5),
        "rw_a_up": nrm((DEPTH, LORA_A, D_RWKV), 0.5 * LORA_A ** -0.5),
        "rw_g_up": nrm((DEPTH, LORA_G, D_RWKV), LORA_G ** -0.5),
        "rw_k_k": 0.85 + nrm((DEPTH, D_RWKV), 0.05),
        "rw_k_a": near_one((DEPTH, D_RWKV)),
        "rw_r_k": nrm((DEPTH, N_HEADS, HEAD_DIM), 0.1),
        "rw_lnx_g": near_one((DEPTH, D_RWKV)),
        "rw_lnx_b": nrm((DEPTH, D_RWKV), 0.02),
        "w_out": nrm((DEPTH, D_MODEL, D_MODEL), D_MODEL ** -0.5),
        "moe_w_group": nrm((DEPTH, D_MODEL, N_GROUPS), D_MODEL ** -0.5),
        "moe_b_group": nrm((DEPTH, N_GROUPS), 0.01),
        "moe_w_expert": nrm((DEPTH, D_MODEL, N_EXPERTS), D_MODEL ** -0.5),
        "moe_b_expert": nrm((DEPTH, N_EXPERTS), 0.01),
        "moe_w_gate": nrm((DEPTH, N_EXPERTS, D_MODEL, D_EXPERT), D_MODEL ** -0.5),
        "moe_w_up": nrm((DEPTH, N_EXPERTS, D_MODEL, D_EXPERT), D_MODEL ** -0.5),
        "moe_w_down": nrm((DEPTH, N_EXPERTS, D_EXPERT, D_MODEL), D_EXPERT ** -0.5),
    }


def reference(x_prompt, x_sample, state_conv, state_shift, state_wkv, c_prompt, c_sample,
              norm_mix_g, norm_ffn_g, final_norm_g, ada_w, ada_b, w_in, mu_shift,
              conv_w, conv_b, conv_ln_g, conv_ln_b,
              rw_w0, rw_w_up, rw_a0, rw_a_up, rw_g_up, rw_k_k, rw_k_a, rw_r_k, rw_lnx_g, rw_lnx_b,
              w_out, moe_w_group, moe_b_group, moe_w_expert, moe_b_expert, moe_w_gate, moe_w_up, moe_w_down):
    P = dict(norm_mix_g=norm_mix_g, norm_ffn_g=norm_ffn_g, final_norm_g=final_norm_g,
             ada_w=ada_w, ada_b=ada_b, w_in=w_in, mu_shift=mu_shift,
             conv_w=conv_w, conv_b=conv_b, conv_ln_g=conv_ln_g, conv_ln_b=conv_ln_b,
             rw_w0=rw_w0, rw_w_up=rw_w_up, rw_a0=rw_a0, rw_a_up=rw_a_up, rw_g_up=rw_g_up,
             rw_k_k=rw_k_k, rw_k_a=rw_k_a, rw_r_k=rw_r_k, rw_lnx_g=rw_lnx_g, rw_lnx_b=rw_lnx_b,
             w_out=w_out, moe_w_group=moe_w_group, moe_b_group=moe_b_group,
             moe_w_expert=moe_w_expert, moe_b_expert=moe_b_expert,
             moe_w_gate=moe_w_gate, moe_w_up=moe_w_up, moe_w_down=moe_w_down)
    bp = x_prompt.shape[0]
    conv0 = jnp.zeros((DEPTH, bp, CONV_BUF, D_CONV), x_prompt.dtype)
    shift0 = jnp.zeros((DEPTH, bp, D_MODEL), x_prompt.dtype)
    wkv0 = jnp.zeros((DEPTH, bp, N_HEADS, HEAD_DIM, HEAD_DIM), jnp.float32)
    y_prompt, conv_p, shift_p, wkv_p = _trunk(x_prompt, c_prompt, shift0, conv0, wkv0, P)
    y_sample, conv_s, shift_s, wkv_s = _trunk(x_sample, c_sample, state_shift, state_conv, state_wkv, P)
    return (y_prompt, y_sample, conv_p, shift_p, wkv_p.astype(state_wkv.dtype),
            conv_s, shift_s, wkv_s.astype(state_wkv.dtype))
```

```python
import functools

import jax
import jax.numpy as jnp
from jax import lax
from jax.experimental import pallas as pl
from jax.experimental.pallas import tpu as pltpu

F32 = jnp.float32
BF16 = jnp.bfloat16

HEAD_DIM = 64
CONV_WIDTH = 31
CONV_BUF = CONV_WIDTH - 1
LORA_W = 64
LORA_A = 64
N_GROUPS = 4
EXPERTS_PER_GROUP = 8
N_EXPERTS = N_GROUPS * EXPERTS_PER_GROUP
RMS_EPS = 1e-6
LN_EPS = 1e-5
GN_EPS = 64e-5

V7X_LANES = 128
V7X_SUBLANES = 8
V7X_VMEM_LIMIT_BYTES = 56 * 1024 * 1024

ROW_TILE = 256
SHORT_SEQ_ROWS = 128
HIST_ROWS = 32
HIST_PAD = HIST_ROWS - CONV_BUF
ROUTER_LANES = V7X_LANES


def _dot(a, b):
    return jnp.dot(a, b, preferred_element_type=F32)


def _sigmoid(x):
    return 1.0 / (1.0 + jnp.exp(-x))


def _rmsnorm(x, g):
    return x * lax.rsqrt(jnp.mean(x * x, axis=-1, keepdims=True) + RMS_EPS) * g


def _params(sem):
    return pltpu.CompilerParams(dimension_semantics=sem, vmem_limit_bytes=V7X_VMEM_LIMIT_BYTES)


def _row_tiling(bsz, t):
    tm = min(t, ROW_TILE)
    bb = max(1, min(bsz, SHORT_SEQ_ROWS // tm))
    assert t % tm == 0 and bsz % bb == 0 and tm % V7X_SUBLANES == 0
    assert bb == 1 or tm == t
    return bb, tm


def _per_seq(ref, bb, n, d):
    p = ref[...]
    if bb == 1:
        return p[0]
    return jnp.broadcast_to(p, (bb, n, d)).reshape(bb * n, d)


def _full_spec(a):
    return pl.BlockSpec(a.shape, lambda b, i: (0,) * a.ndim)


def _ada_kernel(c_ref, w_ref, b_ref, o_ref):
    c = c_ref[...]
    s = c * _sigmoid(c)
    o_ref[0] = _dot(s.astype(BF16), w_ref[0].astype(BF16)) + b_ref[0]


def _modulation(c, ada_w, ada_b):
    depth, d, n = ada_w.shape
    bsz = c.shape[0]
    tn = 1536
    return pl.pallas_call(
        _ada_kernel,
        out_shape=jax.ShapeDtypeStruct((depth, bsz, n), F32),
        grid=(depth, n // tn),
        in_specs=[
            pl.BlockSpec((bsz, d), lambda l, j: (0, 0)),
            pl.BlockSpec((1, d, tn), lambda l, j: (l, 0, j)),
            pl.BlockSpec((1, 1, tn), lambda l, j: (l, 0, j)),
        ],
        out_specs=pl.BlockSpec((1, bsz, tn), lambda l, j: (l, 0, j)),
        compiler_params=_params(("arbitrary", "arbitrary")),
        name="ada_modulation",
    )(c, ada_w, ada_b.reshape(depth, 1, n))


def _mixer_pre_kernel(has_res, bb, tm, d, *refs):
    if has_res:
        xa_ref, xb_ref, gtp_ref, *refs = refs
    else:
        xa_ref, *refs = refs
    (sh_ref, sc_ref, g_ref, win_ref, shift0_ref, conv0_ref, mu_ref, cw_ref, cb_ref, clg_ref, clb_ref,
     w0_ref, lora_ref, a0_ref, gup_ref,
     yag_ref, gb_ref, r_ref, w_ref, k_ref, v_ref, a_ref, convn_ref, shiftn_ref,
     ubuf_ref, z_ref, carry_ref) = refs
    c_plain = 4 * d
    rows = bb * tm
    t = pl.program_id(1)
    nt = pl.num_programs(1)

    x = xa_ref[...].reshape(rows, d)
    if has_res:
        x = x + _per_seq(gtp_ref, bb, tm, d) * xb_ref[...].reshape(rows, d)
    h = _rmsnorm(x, g_ref[...]) * (1.0 + _per_seq(sc_ref, bb, tm, d)) + _per_seq(sh_ref, bb, tm, d)
    proj = _dot(h.astype(BF16), win_ref[...])

    @pl.when(t == 0)
    def _():
        hp = _per_seq(shift0_ref, bb, V7X_SUBLANES, d)
        hp = jnp.broadcast_to(hp, (bb * V7X_SUBLANES, d)).astype(BF16)
        carry_ref[...] = _dot(hp, win_ref[:, c_plain:])
        ubuf_ref[:, pl.ds(0, HIST_PAD), :] = jnp.zeros((bb, HIST_PAD, d), F32)
        ubuf_ref[:, pl.ds(HIST_PAD, CONV_BUF), :] = conv0_ref[...]

    @pl.when(t == nt - 1)
    def _():
        shiftn_ref[...] = h.reshape(bb, tm, d)[:, tm - 1:tm, :]

    cur = proj[:, c_plain:]
    prev = pltpu.roll(cur, 1, 0)
    row = lax.broadcasted_iota(jnp.int32, cur.shape, 0)
    if bb == 1:
        prev = jnp.where(row == 0, carry_ref[pl.ds(0, 1), :], prev)
        carry_ref[pl.ds(0, 1), :] = cur[tm - 1:tm, :]
    else:
        prev = jnp.where((row & (tm - 1)) == 0, carry_ref[...], prev)
    rw = cur + (prev - cur) * mu_ref[...]

    r = rw[:, 0:d]
    k = rw[:, d:2 * d]
    v = rw[:, 2 * d:3 * d]
    xwa = rw[:, 3 * d:3 * d + LORA_W + LORA_A]
    xg = rw[:, 3 * d + LORA_W + LORA_A:]
    lane = lax.broadcasted_iota(jnp.int32, xwa.shape, 1)
    lhs = jnp.where(lane < LORA_W, jnp.tanh(xwa), xwa)
    za = _dot(lhs.astype(BF16), lora_ref[...])
    decay = jnp.exp(-jnp.exp(-0.5) * _sigmoid(w0_ref[...] + za[:, 0:d]))
    a = _sigmoid(a0_ref[...] + za[:, d:])
    gate = _dot(_sigmoid(xg).astype(BF16), gup_ref[...])
    blk = (bb, tm, d)
    r_ref[...] = r.reshape(blk)
    k_ref[...] = k.reshape(blk)
    v_ref[...] = v.reshape(blk)
    w_ref[...] = decay.reshape(blk)
    a_ref[...] = a.reshape(blk)
    gb_ref[...] = (_sigmoid(proj[:, 3 * d:4 * d]) * gate).reshape(blk)

    u = proj[:, 0:d] * _sigmoid(proj[:, d:2 * d])
    ubuf_ref[:, pl.ds(HIST_ROWS, tm), :] = u.reshape(blk)
    rb = min(tm, 32)
    lc = 256
    for b in range(bb):
        for r0 in range(0, tm, rb):
            for c0 in range(0, d, lc):
                acc = jnp.broadcast_to(cb_ref[:, c0:c0 + lc], (rb, lc))
                for tap in range(CONV_WIDTH):
                    acc = acc + (cw_ref[pl.ds(tap, 1), c0:c0 + lc]
                                 * ubuf_ref[b, pl.ds(r0 + HIST_PAD + tap, rb), c0:c0 + lc])
                z_ref[pl.ds(b * tm + r0, rb), c0:c0 + lc] = acc
    z = z_ref[...]
    mu = jnp.mean(z, axis=-1, keepdims=True)
    zc = z - mu
    var = jnp.mean(zc * zc, axis=-1, keepdims=True)
    zn = zc * lax.rsqrt(var + LN_EPS) * clg_ref[...] + clb_ref[...]
    yag_ref[...] = (_sigmoid(proj[:, 2 * d:3 * d]) * (zn * _sigmoid(zn))).reshape(blk)

    @pl.when(t == nt - 1)
    def _():
        convn_ref[...] = ubuf_ref[:, pl.ds(tm + HIST_PAD, CONV_BUF), :]

    if tm >= HIST_ROWS:
        @pl.when(t < nt - 1)
        def _():
            ubuf_ref[:, pl.ds(0, HIST_ROWS), :] = ubuf_ref[:, pl.ds(tm, HIST_ROWS), :]


def _mixer_pre(xa, res, sh, sc, g, win_bf, shift0, conv0, mu, cw, cb, clg, clb, w0, lora_bf, a0, gup_bf):
    bsz, t, d = xa.shape
    bb, tm = _row_tiling(bsz, t)
    assert t == tm or tm >= HIST_ROWS
    assert bb == 1 or tm == V7X_SUBLANES
    c_rwkv = win_bf.shape[1] - 4 * d
    tile = pl.BlockSpec((bb, tm, d), lambda b, i: (b, i, 0))
    perb = pl.BlockSpec((bb, 1, d), lambda b, i: (b, 0, 0))
    hist = pl.BlockSpec((bb, CONV_BUF, d), lambda b, i: (b, 0, 0))
    ins, specs = [xa], [tile]
    if res is not None:
        ins += [res[0], res[1]]
        specs += [tile, perb]
    ins += [sh, sc, g, win_bf, shift0, conv0, mu, cw, cb, clg, clb, w0, lora_bf, a0, gup_bf]
    specs += [perb, perb, _full_spec(g),
              pl.BlockSpec(win_bf.shape, lambda b, i: (0, 0), pipeline_mode=pl.Buffered(1)),
              perb, hist] + [_full_spec(p) for p in (mu, cw, cb, clg, clb, w0, lora_bf, a0, gup_bf)]
    act = jax.ShapeDtypeStruct((bsz, t, d), F32)
    return pl.pallas_call(
        functools.partial(_mixer_pre_kernel, res is not None, bb, tm, d),
        out_shape=[act] * 7 + [jax.ShapeDtypeStruct((bsz, CONV_BUF, d), F32), jax.ShapeDtypeStruct((bsz, 1, d), F32)],
        grid=(bsz // bb, t // tm),
        in_specs=specs,
        out_specs=[tile] * 7 + [hist, perb],
        scratch_shapes=[pltpu.VMEM((bb, tm + HIST_ROWS, d), F32), pltpu.VMEM((bb * tm, d), F32),
                        pltpu.VMEM((bb * V7X_SUBLANES, c_rwkv), F32)],
        compiler_params=_params(("arbitrary", "arbitrary")),
        name="mixer_pre",
    )(*ins)


def _wkv_kernel(tc, r_ref, w_ref, k_ref, v_ref, a_ref, kkp_ref, kap_ref, rkp_ref, lng_ref, lnb_ref, s0_ref,
                y_ref, s_ref, o_ref):
    n = HEAD_DIM

    @pl.when(pl.program_id(1) == 0)
    def _():
        s_ref[...] = s0_ref[...]

    def step(tt, carry):
        r_t = r_ref[tt]
        k_t = k_ref[tt]
        a_t = a_ref[tt]
        w_t = w_ref[tt]
        kk = k_t * kkp_ref[...]
        kk = kk * lax.rsqrt(jnp.sum(kk * kk, axis=0, keepdims=True) + 1e-12)
        b_t = kk * a_t
        k2 = k_t * (1.0 + (a_t - 1.0) * kap_ref[...])

        def row(i, c):
            si = s_ref[i]
            sa = jnp.sum(si * kk, axis=0, keepdims=True)
            vi = v_ref[tt, pl.ds(i, 1), :]
            sn = si * w_t - sa * b_t + vi * k2
            s_ref[i] = sn
            o_ref[pl.ds(i, 1), :] = jnp.sum(sn * r_t, axis=0, keepdims=True)
            return c

        lax.fori_loop(0, n, row, 0, unroll=8)
        o = o_ref[...]
        mu = jnp.mean(o, axis=0, keepdims=True)
        oc = o - mu
        var = jnp.mean(oc * oc, axis=0, keepdims=True)
        yn = oc * lax.rsqrt(var + GN_EPS) * lng_ref[...] + lnb_ref[...]
        bonus = jnp.sum(r_t * k2 * rkp_ref[...], axis=0, keepdims=True) * v_ref[tt]
        y_ref[tt] = yn + bonus
        return carry

    lax.fori_loop(0, tc, step, 0)


def _wkv_scan(r, w, k, v, a, kkp, kap, rkp, lng, lnb, s0):
    t, n, nl = r.shape
    tc = min(t, 32)
    lb = V7X_LANES
    seq = pl.BlockSpec((tc, n, lb), lambda l, i: (i, 0, l))
    par = pl.BlockSpec((n, lb), lambda l, i: (0, 0))
    st = pl.BlockSpec((n, n, lb), lambda l, i: (0, 0, l))
    return pl.pallas_call(
        functools.partial(_wkv_kernel, tc),
        out_shape=[jax.ShapeDtypeStruct((t, n, nl), F32), jax.ShapeDtypeStruct((n, n, nl), F32)],
        grid=(nl // lb, t // tc),
        in_specs=[seq] * 5 + [par] * 5 + [st],
        out_specs=[seq, st],
        scratch_shapes=[pltpu.VMEM((n, lb), F32)],
        compiler_params=_params(("arbitrary", "arbitrary")),
        name="wkv_scan",
    )(r, w, k, v, a, kkp, kap, rkp, lng, lnb, s0)


def _mixer_post_kernel(has_res, bb, tm, d, *refs):
    if has_res:
        xa_ref, xb_ref, gtp_ref, *refs = refs
    else:
        xa_ref, *refs = refs
    (yag_ref, gb_ref, y_ref, gt1_ref, sh_ref, sc_ref, g_ref, wout_ref, wrh_ref, wrl_ref, br_ref,
     x1_ref, h2_ref, comb_ref) = refs
    rows = bb * tm
    x = xa_ref[...].reshape(rows, d)
    if has_res:
        x = x + _per_seq(gtp_ref, bb, tm, d) * xb_ref[...].reshape(rows, d)
    merged = (yag_ref[...] + gb_ref[...] * y_ref[...]).reshape(rows, d)
    x1 = x + _per_seq(gt1_ref, bb, tm, d) * _dot(merged.astype(BF16), wout_ref[...])
    x1_ref[...] = x1.reshape(bb, tm, d)
    h2 = _rmsnorm(x1, g_ref[...]) * (1.0 + _per_seq(sc_ref, bb, tm, d)) + _per_seq(sh_ref, bb, tm, d)
    h_hi = h2.astype(BF16)
    h2_ref[...] = h_hi.reshape(bb, tm, d)
    h_lo = (h2 - h_hi.astype(F32)).astype(BF16)
    logits = _dot(h_hi, wrh_ref[...]) + (_dot(h_lo, wrh_ref[...]) + _dot(h_hi, wrl_ref[...])) + br_ref[...]

    lane_i = lax.broadcasted_iota(jnp.int32, logits.shape, 1)
    lane = lane_i.astype(F32)
    grp_of_lane = lax.shift_right_logical(lane_i, 3).astype(F32)
    neg = -jnp.inf
    none = 1e9
    is_g = (lane_i >= N_EXPERTS) & (lane_i < N_EXPERTS + N_GROUPS)
    glv = jnp.where(is_g, logits, neg)
    gmax = jnp.max(glv, axis=-1, keepdims=True)
    gidx = jnp.min(jnp.where(is_g & (glv == gmax), lane - N_EXPERTS, none), axis=-1, keepdims=True)
    pg = 1.0 / jnp.sum(jnp.where(is_g, jnp.exp(glv - gmax), 0.0), axis=-1, keepdims=True)
    in_grp = (lane_i < N_EXPERTS) & (grp_of_lane == gidx)
    ev = jnp.where(in_grp, logits, neg)
    v1 = jnp.max(ev, axis=-1, keepdims=True)
    i1 = jnp.min(jnp.where(in_grp & (ev == v1), lane, none), axis=-1, keepdims=True)
    ev2 = jnp.where(lane == i1, neg, ev)
    v2 = jnp.max(ev2, axis=-1, keepdims=True)
    i2 = jnp.min(jnp.where(in_grp & (ev2 == v2) & (lane != i1), lane, none), axis=-1, keepdims=True)
    e2 = jnp.exp(v2 - v1)
    den = 1.0 + e2
    comb = jnp.where(lane == i1, (1.0 / den) * pg, 0.0) + jnp.where(lane == i2, (e2 / den) * pg, 0.0)
    comb_ref[...] = comb.reshape(bb, tm, ROUTER_LANES)


def _mixer_post(xa, res, yag, gb, y, gt1, sh, sc, g, wout_bf, wr_hi, wr_lo, br):
    bsz, t, d = xa.shape
    bb, tm = _row_tiling(bsz, t)
    tile = pl.BlockSpec((bb, tm, d), lambda b, i: (b, i, 0))
    perb = pl.BlockSpec((bb, 1, d), lambda b, i: (b, 0, 0))
    ins, specs = [xa], [tile]
    if res is not None:
        ins += [res[0], res[1]]
        specs += [tile, perb]
    ins += [yag, gb, y, gt1, sh, sc, g, wout_bf, wr_hi, wr_lo, br]
    specs += [tile, tile, tile, perb, perb, perb] + [_full_spec(p) for p in (g, wout_bf, wr_hi, wr_lo, br)]
    return pl.pallas_call(
        functools.partial(_mixer_post_kernel, res is not None, bb, tm, d),
        out_shape=[jax.ShapeDtypeStruct((bsz, t, d), F32), jax.ShapeDtypeStruct((bsz, t, d), BF16),
                   jax.ShapeDtypeStruct((bsz, t, ROUTER_LANES), F32)],
        grid=(bsz // bb, t // tm),
        in_specs=specs,
        out_specs=[tile, tile, pl.BlockSpec((bb, tm, ROUTER_LANES), lambda b, i: (b, i, 0))],
        compiler_params=_params(("arbitrary", "arbitrary")),
        name="mixer_post",
    )(*ins)


def _moe_kernel(h_ref, comb_ref, wg_ref, wu_ref, wd_ref, o_ref):
    e = pl.program_id(1)

    @pl.when(e == 0)
    def _():
        o_ref[...] = jnp.zeros_like(o_ref)

    hb = h_ref[...]
    hg = _dot(hb, wg_ref[0])
    hu = _dot(hb, wu_ref[0])
    comb = comb_ref[...]
    lane = lax.broadcasted_iota(jnp.int32, comb.shape, 1)
    ce = jnp.sum(jnp.where(lane == e, comb, 0.0), axis=-1, keepdims=True)
    act = hg * _sigmoid(hg) * hu * ce
    o_ref[...] += _dot(act.astype(BF16), wd_ref[0])


def _moe(h2, comb, wg_bf, wu_bf, wd_bf):
    n, d = h2.shape
    ne, _, f = wg_bf.shape
    tm = min(n, 1024)
    return pl.pallas_call(
        _moe_kernel,
        out_shape=jax.ShapeDtypeStruct((n, d), F32),
        grid=(n // tm, ne),
        in_specs=[
            pl.BlockSpec((tm, d), lambda i, e: (i, 0)),
            pl.BlockSpec((tm, ROUTER_LANES), lambda i, e: (i, 0)),
            pl.BlockSpec((1, d, f), lambda i, e: (e, 0, 0)),
            pl.BlockSpec((1, d, f), lambda i, e: (e, 0, 0)),
            pl.BlockSpec((1, f, d), lambda i, e: (e, 0, 0)),
        ],
        out_specs=pl.BlockSpec((tm, d), lambda i, e: (i, 0)),
        compiler_params=_params(("arbitrary", "arbitrary")),
        name="moe_experts",
    )(h2, comb, wg_bf, wu_bf, wd_bf)


def _final_kernel(bb, tm, d, xa_ref, xb_ref, gtp_ref, g_ref, o_ref):
    rows = bb * tm
    x = xa_ref[...].reshape(rows, d) + _per_seq(gtp_ref, bb, tm, d) * xb_ref[...].reshape(rows, d)
    o_ref[...] = _rmsnorm(x, g_ref[...]).reshape(bb, tm, d)


def _final_norm(xa, xb, gtp, g):
    bsz, t, d = xa.shape
    bb, tm = _row_tiling(bsz, t)
    tile = pl.BlockSpec((bb, tm, d), lambda b, i: (b, i, 0))
    return pl.pallas_call(
        functools.partial(_final_kernel, bb, tm, d),
        out_shape=jax.ShapeDtypeStruct((bsz, t, d), F32),
        grid=(bsz // bb, t // tm),
        in_specs=[tile, tile, pl.BlockSpec((bb, 1, d), lambda b, i: (b, 0, 0)), _full_spec(g)],
        out_specs=tile,
        compiler_params=_params(("arbitrary", "arbitrary")),
        name="final_norm",
    )(xa, xb, gtp, g)


def _to_lanes(z, n_heads):
    bsz, t, _ = z.shape
    return z.reshape(bsz, t, n_heads, HEAD_DIM).transpose(1, 3, 0, 2).reshape(t, HEAD_DIM, bsz * n_heads)


def _from_lanes(z, bsz, n_heads):
    t = z.shape[0]
    return z.reshape(t, HEAD_DIM, bsz, n_heads).transpose(2, 0, 3, 1).reshape(bsz, t, n_heads * HEAD_DIM)


def _head_param(p, n_heads):
    return jnp.tile(p.reshape(n_heads, HEAD_DIM).T, (1, V7X_LANES // n_heads))


def _trunk(x, c, shift0, conv0, wkv0, W):
    bsz, t, d = x.shape
    n_heads = d // HEAD_DIM
    depth = W["w_in_bf"].shape[0]
    mod = _modulation(c, W["ada_w"], W["ada_b"])
    bufs, shifts, states = [], [], []
    res = None
    xa = x
    for l in range(depth):
        sh1, sc1, gt1, sh2, sc2, gt2 = (mod[l, :, i * d:(i + 1) * d][:, None, :] for i in range(6))
        row = lambda p: p[l][None, :]
        yag, gb, r, w, k, v, a, convn, shiftn = _mixer_pre(
            xa, res, sh1, sc1, row(W["norm_mix_g"]), W["w_in_bf"][l], shift0[l][:, None, :], conv0[l],
            row(W["mu_shift"]), W["conv_w"][l], row(W["conv_b"]), row(W["conv_ln_g"]), row(W["conv_ln_b"]),
            row(W["rw_w0"]), W["lora_bf"][l], row(W["rw_a0"]), W["g_up_bf"][l])
        hp = lambda p: _head_param(p, n_heads)
        s0 = wkv0[l].transpose(2, 3, 0, 1).reshape(HEAD_DIM, HEAD_DIM, bsz * n_heads)
        y_l, s_l = _wkv_scan(*(_to_lanes(z, n_heads) for z in (r, w, k, v, a)),
                             hp(W["rw_k_k"][l]), hp(W["rw_k_a"][l]), hp(W["rw_r_k"][l].reshape(-1)),
                             hp(W["rw_lnx_g"][l]), hp(W["rw_lnx_b"][l]), s0)
        y = _from_lanes(y_l, bsz, n_heads)
        states.append(s_l.reshape(HEAD_DIM, HEAD_DIM, bsz, n_heads).transpose(2, 3, 0, 1))
        x1, h2, comb = _mixer_post(xa, res, yag, gb, y, gt1, sh2, sc2, row(W["norm_ffn_g"]), W["w_out_bf"][l],
                                   W["wr_hi"][l], W["wr_lo"][l], W["br"][l])
        moe = _moe(h2.reshape(bsz * t, d), comb.reshape(bsz * t, ROUTER_LANES),
                   W["moe_wg_bf"][l], W["moe_wu_bf"][l], W["moe_wd_bf"][l]).reshape(bsz, t, d)
        xa, res = x1, (moe, gt2)
        bufs.append(convn)
        shifts.append(shiftn[:, 0, :])
    y_out = _final_norm(xa, res[0], res[1], W["final_norm_g"][None, :])
    return y_out, jnp.stack(bufs), jnp.stack(shifts), jnp.stack(states)


def kernel(x_prompt, x_sample, state_conv, state_shift, state_wkv, c_prompt, c_sample, norm_mix_g, norm_ffn_g, final_norm_g, ada_w, ada_b, w_in, mu_shift, conv_w, conv_b, conv_ln_g, conv_ln_b, rw_w0, rw_w_up, rw_a0, rw_a_up, rw_g_up, rw_k_k, rw_k_a, rw_r_k, rw_lnx_g, rw_lnx_b, w_out, moe_w_group, moe_b_group, moe_w_expert, moe_b_expert, moe_w_gate, moe_w_up, moe_w_down):
    depth, d, _ = w_in.shape
    zw = jnp.zeros_like(rw_w_up)
    lora = jnp.concatenate([jnp.concatenate([rw_w_up, zw], axis=2), jnp.concatenate([zw, rw_a_up], axis=2)], axis=1)
    pad = ROUTER_LANES - N_EXPERTS - N_GROUPS
    wr = jnp.concatenate([moe_w_expert, moe_w_group, jnp.zeros((depth, d, pad), F32)], axis=2)
    wr_hi = wr.astype(BF16)
    br = jnp.concatenate([moe_b_expert, moe_b_group, jnp.zeros((depth, pad), F32)], axis=1)[:, None, :]
    W = dict(norm_mix_g=norm_mix_g, norm_ffn_g=norm_ffn_g, final_norm_g=final_norm_g, ada_w=ada_w, ada_b=ada_b,
             w_in_bf=w_in.astype(BF16), mu_shift=mu_shift, conv_w=conv_w, conv_b=conv_b, conv_ln_g=conv_ln_g,
             conv_ln_b=conv_ln_b, rw_w0=rw_w0, lora_bf=lora.astype(BF16), rw_a0=rw_a0, g_up_bf=rw_g_up.astype(BF16),
             rw_k_k=rw_k_k, rw_k_a=rw_k_a, rw_r_k=rw_r_k, rw_lnx_g=rw_lnx_g, rw_lnx_b=rw_lnx_b,
             w_out_bf=w_out.astype(BF16), wr_hi=wr_hi, wr_lo=(wr - wr_hi.astype(F32)).astype(BF16), br=br,
             moe_wg_bf=moe_w_gate.astype(BF16), moe_wu_bf=moe_w_up.astype(BF16), moe_wd_bf=moe_w_down.astype(BF16))
    bp = x_prompt.shape[0]
    n_heads = d // HEAD_DIM
    conv0 = jnp.zeros((depth, bp, CONV_BUF, d), x_prompt.dtype)
    shift0 = jnp.zeros((depth, bp, d), x_prompt.dtype)
    wkv0 = jnp.zeros((depth, bp, n_heads, HEAD_DIM, HEAD_DIM), F32)
    y_p, conv_p, shift_p, wkv_p = _trunk(x_prompt, c_prompt, shift0, conv0, wkv0, W)
    y_s, conv_s, shift_s, wkv_s = _trunk(x_sample, c_sample, state_shift, state_conv, state_wkv, W)
    return (y_p, y_s, conv_p, shift_p, wkv_p.astype(state_wkv.dtype), conv_s, shift_s, wkv_s.astype(state_wkv.dtype))
```

```python
import functools

import jax
import jax.numpy as jnp
from jax import lax
from jax.experimental import pallas as pl
from jax.experimental.pallas import tpu as pltpu

F32 = jnp.float32
BF16 = jnp.bfloat16

HEAD_DIM = 64
CONV_WIDTH = 31
CONV_BUF = CONV_WIDTH - 1
LORA_W = 64
LORA_A = 64
N_GROUPS = 4
EXPERTS_PER_GROUP = 8
N_EXPERTS = N_GROUPS * EXPERTS_PER_GROUP
RMS_EPS = 1e-6
LN_EPS = 1e-5
GN_EPS = 64e-5

V7X_LANES = 128
V7X_SUBLANES = 8
V7X_VMEM_LIMIT_BYTES = 56 * 1024 * 1024

ROW_TILE = 256
SHORT_SEQ_ROWS = 128
HIST_ROWS = 32
HIST_PAD = HIST_ROWS - CONV_BUF
CONV_LANE_CHUNK = 256
WKV_CHUNK = 64
WKV_GROUP_HEADS = 4
WKV_GROUP = WKV_GROUP_HEADS * HEAD_DIM
ROUTER_LANES = V7X_LANES


def _dot(a, b):
    return jnp.dot(a, b, preferred_element_type=F32)


def _sigmoid(x):
    return 1.0 / (1.0 + jnp.exp(-x))


def _rmsnorm(x, g):
    return x * lax.rsqrt(jnp.mean(x * x, axis=-1, keepdims=True) + RMS_EPS) * g


def _params(sem):
    return pltpu.CompilerParams(dimension_semantics=sem, vmem_limit_bytes=V7X_VMEM_LIMIT_BYTES)


def _row_tiling(bsz, t):
    tm = min(t, ROW_TILE)
    bb = max(1, min(bsz, SHORT_SEQ_ROWS // tm))
    assert t % tm == 0 and bsz % bb == 0 and tm % V7X_SUBLANES == 0
    assert bb == 1 or tm == t
    return bb, tm


def _per_seq(ref, bb, n, d):
    p = ref[...]
    if bb == 1:
        return p[0]
    return jnp.broadcast_to(p, (bb, n, d)).reshape(bb * n, d)


def _full_spec(a):
    return pl.BlockSpec(a.shape, lambda b, i: (0,) * a.ndim)


def _ada_kernel(c_ref, w_ref, b_ref, o_ref):
    c = c_ref[...]
    s = c * _sigmoid(c)
    o_ref[0] = _dot(s.astype(BF16), w_ref[0].astype(BF16)) + b_ref[0]


def _modulation(c, ada_w, ada_b):
    depth, d, n = ada_w.shape
    bsz = c.shape[0]
    tn = 1536
    return pl.pallas_call(
        _ada_kernel,
        out_shape=jax.ShapeDtypeStruct((depth, bsz, n), F32),
        grid=(depth, n // tn),
        in_specs=[
            pl.BlockSpec((bsz, d), lambda l, j: (0, 0)),
            pl.BlockSpec((1, d, tn), lambda l, j: (l, 0, j)),
            pl.BlockSpec((1, 1, tn), lambda l, j: (l, 0, j)),
        ],
        out_specs=pl.BlockSpec((1, bsz, tn), lambda l, j: (l, 0, j)),
        compiler_params=_params(("arbitrary", "arbitrary")),
        name="ada_modulation",
    )(c, ada_w, ada_b.reshape(depth, 1, n))


def _mixer_pre_kernel(has_res, bb, tm, d, *refs):
    if has_res:
        xa_ref, xb_ref, gtp_ref, *refs = refs
    else:
        xa_ref, *refs = refs
    (sh_ref, sc_ref, g_ref, win_ref, shift0_ref, conv0_ref, mu_ref, cw_ref, cb_ref, clg_ref, clb_ref,
     w0_ref, lora_ref, a0_ref, gup_ref,
     yag_ref, gb_ref, r_ref, lw_ref, k_ref, v_ref, a_ref, convn_ref, shiftn_ref,
     ubuf_ref, z_ref, carry_ref, shifted_ref) = refs
    c_plain = 4 * d
    rows = bb * tm
    t = pl.program_id(1)
    nt = pl.num_programs(1)

    x = xa_ref[...].reshape(rows, d)
    if has_res:
        x = x + _per_seq(gtp_ref, bb, tm, d) * xb_ref[...].reshape(rows, d)
    h = _rmsnorm(x, g_ref[...]) * (1.0 + _per_seq(sc_ref, bb, tm, d)) + _per_seq(sh_ref, bb, tm, d)
    proj = _dot(h.astype(BF16), win_ref[...])

    @pl.when(t == 0)
    def _():
        hp = _per_seq(shift0_ref, bb, V7X_SUBLANES, d)
        hp = jnp.broadcast_to(hp, (bb * V7X_SUBLANES, d)).astype(BF16)
        carry_ref[...] = _dot(hp, win_ref[:, c_plain:])
        ubuf_ref[:, pl.ds(0, HIST_PAD), :] = jnp.zeros((bb, HIST_PAD, d), F32)
        ubuf_ref[:, pl.ds(HIST_PAD, CONV_BUF), :] = conv0_ref[...]

    @pl.when(t == nt - 1)
    def _():
        shiftn_ref[...] = h.reshape(bb, tm, d)[:, tm - 1:tm, :]

    cur = proj[:, c_plain:]
    prev = pltpu.roll(cur, 1, 0)
    row = lax.broadcasted_iota(jnp.int32, cur.shape, 0)
    if bb == 1:
        prev = jnp.where(row == 0, carry_ref[pl.ds(0, 1), :], prev)
        carry_ref[pl.ds(0, 1), :] = cur[tm - 1:tm, :]
    else:
        prev = jnp.where((row & (tm - 1)) == 0, carry_ref[...], prev)
    rw = cur + (prev - cur) * mu_ref[...]

    r = rw[:, 0:d]
    k = rw[:, d:2 * d]
    v = rw[:, 2 * d:3 * d]
    xwa = rw[:, 3 * d:3 * d + LORA_W + LORA_A]
    xg = rw[:, 3 * d + LORA_W + LORA_A:]
    lane = lax.broadcasted_iota(jnp.int32, xwa.shape, 1)
    lhs = jnp.where(lane < LORA_W, jnp.tanh(xwa), xwa)
    za = _dot(lhs.astype(BF16), lora_ref[...])
    log_decay = -jnp.exp(-0.5) * _sigmoid(w0_ref[...] + za[:, 0:d])
    a = _sigmoid(a0_ref[...] + za[:, d:])
    gate = _dot(_sigmoid(xg).astype(BF16), gup_ref[...])
    blk = (bb, tm, d)
    r_ref[...] = r.reshape(blk)
    k_ref[...] = k.reshape(blk)
    v_ref[...] = v.reshape(blk)
    lw_ref[...] = log_decay.reshape(blk)
    a_ref[...] = a.reshape(blk)
    gb_ref[...] = (_sigmoid(proj[:, 3 * d:4 * d]) * gate).reshape(blk)

    u = proj[:, 0:d] * _sigmoid(proj[:, d:2 * d])
    ubuf_ref[:, pl.ds(HIST_ROWS, tm), :] = u.reshape(blk)
    rb = min(tm, 32)
    lc = CONV_LANE_CHUNK
    span = tm + HIST_ROWS - V7X_SUBLANES
    for b in range(bb):
        for c0 in range(0, d, lc):
            for s in range(1, V7X_SUBLANES):
                shifted_ref[s - 1] = ubuf_ref[b, pl.ds(s, span), c0:c0 + lc]
            for r0 in range(0, tm, rb):
                acc = jnp.broadcast_to(cb_ref[:, c0:c0 + lc], (rb, lc))
                for tap in range(CONV_WIDTH):
                    s = (HIST_PAD + tap) % V7X_SUBLANES
                    base = r0 + HIST_PAD + tap - s
                    if s == 0:
                        src = ubuf_ref[b, pl.ds(base, rb), c0:c0 + lc]
                    else:
                        src = shifted_ref[s - 1, pl.ds(base, rb), :]
                    acc = acc + cw_ref[pl.ds(tap, 1), c0:c0 + lc] * src
                z_ref[pl.ds(b * tm + r0, rb), c0:c0 + lc] = acc
    z = z_ref[...]
    mu = jnp.mean(z, axis=-1, keepdims=True)
    zc = z - mu
    var = jnp.mean(zc * zc, axis=-1, keepdims=True)
    zn = zc * lax.rsqrt(var + LN_EPS) * clg_ref[...] + clb_ref[...]
    yag_ref[...] = (_sigmoid(proj[:, 2 * d:3 * d]) * (zn * _sigmoid(zn))).reshape(blk)

    @pl.when(t == nt - 1)
    def _():
        convn_ref[...] = ubuf_ref[:, pl.ds(tm + HIST_PAD, CONV_BUF), :]

    if tm >= HIST_ROWS:
        @pl.when(t < nt - 1)
        def _():
            ubuf_ref[:, pl.ds(0, HIST_ROWS), :] = ubuf_ref[:, pl.ds(tm, HIST_ROWS), :]


def _mixer_pre(xa, res, sh, sc, g, win_bf, shift0, conv0, mu, cw, cb, clg, clb, w0, lora_bf, a0, gup_bf):
    bsz, t, d = xa.shape
    bb, tm = _row_tiling(bsz, t)
    assert t == tm or tm >= HIST_ROWS
    assert bb == 1 or tm == V7X_SUBLANES
    c_rwkv = win_bf.shape[1] - 4 * d
    tile = pl.BlockSpec((bb, tm, d), lambda b, i: (b, i, 0))
    perb = pl.BlockSpec((bb, 1, d), lambda b, i: (b, 0, 0))
    hist = pl.BlockSpec((bb, CONV_BUF, d), lambda b, i: (b, 0, 0))
    ins, specs = [xa], [tile]
    if res is not None:
        ins += [res[0], res[1]]
        specs += [tile, perb]
    ins += [sh, sc, g, win_bf, shift0, conv0, mu, cw, cb, clg, clb, w0, lora_bf, a0, gup_bf]
    specs += [perb, perb, _full_spec(g),
              pl.BlockSpec(win_bf.shape, lambda b, i: (0, 0), pipeline_mode=pl.Buffered(1)),
              perb, hist] + [_full_spec(p) for p in (mu, cw, cb, clg, clb, w0, lora_bf, a0, gup_bf)]
    act = jax.ShapeDtypeStruct((bsz, t, d), F32)
    return pl.pallas_call(
        functools.partial(_mixer_pre_kernel, res is not None, bb, tm, d),
        out_shape=[act] * 7 + [jax.ShapeDtypeStruct((bsz, CONV_BUF, d), F32), jax.ShapeDtypeStruct((bsz, 1, d), F32)],
        grid=(bsz // bb, t // tm),
        in_specs=specs,
        out_specs=[tile] * 7 + [hist, perb],
        scratch_shapes=[pltpu.VMEM((bb, tm + HIST_ROWS, d), F32), pltpu.VMEM((bb * tm, d), F32),
                        pltpu.VMEM((bb * V7X_SUBLANES, c_rwkv), F32),
                        pltpu.VMEM((V7X_SUBLANES - 1, tm + HIST_ROWS - V7X_SUBLANES, CONV_LANE_CHUNK), F32)],
        compiler_params=_params(("arbitrary", "arbitrary")),
        name="mixer_pre",
    )(*ins)


def _wkv_kernel(tc, r_ref, lw_ref, k_ref, v_ref, a_ref, kkp_ref, kap_ref, rkp_ref, lng_ref, lnb_ref, s0_ref,
                y_ref, s_ref, o_ref):
    n = HEAD_DIM

    @pl.when(pl.program_id(1) == 0)
    def _():
        s_ref[...] = s0_ref[...]

    def step(tt, carry):
        r_t = r_ref[tt]
        k_t = k_ref[tt]
        a_t = a_ref[tt]
        w_t = jnp.exp(lw_ref[tt])
        kk = k_t * kkp_ref[...]
        kk = kk * lax.rsqrt(jnp.sum(kk * kk, axis=0, keepdims=True) + 1e-12)
        b_t = kk * a_t
        k2 = k_t * (1.0 + (a_t - 1.0) * kap_ref[...])

        def row(i, c):
            si = s_ref[i]
            sa = jnp.sum(si * kk, axis=0, keepdims=True)
            vi = v_ref[tt, pl.ds(i, 1), :]
            sn = si * w_t - sa * b_t + vi * k2
            s_ref[i] = sn
            o_ref[pl.ds(i, 1), :] = jnp.sum(sn * r_t, axis=0, keepdims=True)
            return c

        lax.fori_loop(0, n, row, 0, unroll=8)
        o = o_ref[...]
        mu = jnp.mean(o, axis=0, keepdims=True)
        oc = o - mu
        var = jnp.mean(oc * oc, axis=0, keepdims=True)
        yn = oc * lax.rsqrt(var + GN_EPS) * lng_ref[...] + lnb_ref[...]
        bonus = jnp.sum(r_t * k2 * rkp_ref[...], axis=0, keepdims=True) * v_ref[tt]
        y_ref[tt] = yn + bonus
        return carry

    lax.fori_loop(0, tc, step, 0)


def _wkv_scan(r, w, k, v, a, kkp, kap, rkp, lng, lnb, s0):
    t, n, nl = r.shape
    tc = min(t, 32)
    lb = V7X_LANES
    seq = pl.BlockSpec((tc, n, lb), lambda l, i: (i, 0, l))
    par = pl.BlockSpec((n, lb), lambda l, i: (0, 0))
    st = pl.BlockSpec((n, n, lb), lambda l, i: (0, 0, l))
    return pl.pallas_call(
        functools.partial(_wkv_kernel, tc),
        out_shape=[jax.ShapeDtypeStruct((t, n, nl), F32), jax.ShapeDtypeStruct((n, n, nl), F32)],
        grid=(nl // lb, t // tc),
        in_specs=[seq] * 5 + [par] * 5 + [st],
        out_specs=[seq, st],
        scratch_shapes=[pltpu.VMEM((n, lb), F32)],
        compiler_params=_params(("arbitrary", "arbitrary")),
        name="wkv_scan",
    )(r, w, k, v, a, kkp, kap, rkp, lng, lnb, s0)


def _bmm(a, b):
    return jnp.einsum("gck,gkn->gcn", a, b, preferred_element_type=F32)


def _bmm_nt(a, b):
    return jnp.einsum("gck,gnk->gcn", a, b, preferred_element_type=F32)


def _bmm_tn(a, b):
    return jnp.einsum("gtc,gtn->gcn", a, b, preferred_element_type=F32)


def _split_bf16(x):
    hi = x.astype(BF16)
    return hi, (x - hi.astype(F32)).astype(BF16)


def _wkv_chunk_kernel(tt, d, r_ref, lw_ref, k_ref, v_ref, a_ref, kkp_ref, kap_ref, rkp_ref, lng_ref, lnb_ref,
                      ones_ref, tri_ref, y_ref, sout_ref, s_ref):
    c = WKV_CHUNK
    gw = WKV_GROUP
    ng = d // gw
    nc = tt // c
    n_items = nc * ng
    t = pl.program_id(1)
    nt = pl.num_programs(1)

    @pl.when(t == 0)
    def _():
        s_ref[...] = jnp.zeros_like(s_ref)

    lane = lax.broadcasted_iota(jnp.int32, (1, c, gw), 2)
    rowi = lax.broadcasted_iota(jnp.int32, (1, c, gw), 1)
    head_of_lane = lax.shift_right_logical(lane, 6)
    hmask = [head_of_lane == h for h in range(WKV_GROUP_HEADS)]
    tcol = lane & (c - 1)
    strict = tcol < rowi
    incl = tcol <= rowi
    bl = lax.broadcasted_iota(jnp.int32, (1, gw, gw), 2)
    br = lax.broadcasted_iota(jnp.int32, (1, gw, gw), 1)
    blockmask = lax.shift_right_logical(bl, 6) == lax.shift_right_logical(br, 6)

    def items(ref):
        x = ref[0].reshape(nc, c, d)
        return jnp.stack([x[:, :, g * gw:(g + 1) * gw] for g in range(ng)], axis=1).reshape(n_items, c, gw)

    def per_group(ref):
        p = jnp.stack([ref[:, g * gw:(g + 1) * gw] for g in range(ng)], axis=0)
        return jnp.broadcast_to(p[None], (nc, ng, 1, gw)).reshape(n_items, 1, gw)

    def bd(xf):
        return jnp.concatenate([jnp.where(hmask[h], xf, 0.0) for h in range(WKV_GROUP_HEADS)], axis=1).astype(BF16)

    def segsum(x):
        shp = x.shape
        hi, lo = _split_bf16(x.reshape(-1, gw))
        return (_dot(hi, ones_ref[...]) + _dot(lo, ones_ref[...])).reshape(shp)

    r, lw, k, v, a = (items(ref) for ref in (r_ref, lw_ref, k_ref, v_ref, a_ref))
    kk = k * per_group(kkp_ref)
    kk = kk * lax.rsqrt(segsum(kk * kk) + 1e-12)
    b = kk * a
    k2 = k * (1.0 + (a - 1.0) * per_group(kap_ref))
    lw_hi, lw_lo = _split_bf16(lw)
    tri = jnp.broadcast_to(tri_ref[...][None], (n_items, c, c))
    cum = _bmm(tri, lw_hi) + _bmm(tri, lw_lo)
    cl = cum[:, c - 1:c, :]
    at = -(kk * jnp.exp(cum - lw))
    rt = r * jnp.exp(cum)
    winv = jnp.exp(-cum)
    wrem = jnp.exp(cl - cum)
    wc = jnp.exp(cl)
    lhs = jnp.concatenate([at, rt], axis=1).astype(BF16)
    rhs = jnp.concatenate([bd(b * winv), bd(k2 * winv)], axis=1)
    sc = _bmm_nt(lhs, rhs)
    l_ab = jnp.where(strict, sc[:, 0:c, 0:gw], 0.0)
    l_ak = jnp.where(strict, sc[:, 0:c, gw:], 0.0)
    l_rb = jnp.where(incl, sc[:, c:, 0:gw], 0.0)
    l_rk = jnp.where(incl, sc[:, c:, gw:], 0.0)
    pk = l_ab
    tm = l_ab
    pbd = bd(pk)
    for _ in range(c.bit_length() - 2):
        pk = _bmm(pk.astype(BF16), pbd)
        pbd = bd(pk)
        tm = tm + pk + _bmm(tm.astype(BF16), pbd)
    tmb = tm.astype(BF16)
    vbd = bd(v)
    ah = at + _bmm(tmb, bd(at))
    x1 = _bmm(l_ak.astype(BF16), vbd)
    gm = x1 + _bmm(tmb, bd(x1))
    l_rbb = l_rb.astype(BF16)
    rh = (rt + _bmm(l_rbb, bd(ah))).astype(BF16)
    o_ind = _bmm(jnp.concatenate([l_rbb, l_rk.astype(BF16)], axis=2), jnp.concatenate([bd(gm), vbd], axis=1))
    bb = (b * wrem).astype(BF16)
    kb = (k2 * wrem).astype(BF16)
    m = jnp.where(blockmask, _bmm_tn(ah.astype(BF16), bb), 0.0).astype(BF16)
    n = jnp.where(blockmask, _bmm_tn(jnp.concatenate([gm, v], axis=1).astype(BF16),
                                     jnp.concatenate([bb, kb], axis=1)), 0.0)

    s = s_ref[...]
    outs = []
    for ci in range(nc):
        sl = slice(ci * ng, (ci + 1) * ng)
        sb = s.astype(BF16)
        outs.append(_bmm_nt(rh[sl], sb) + o_ind[sl])
        s = s * wc[sl] + _bmm(sb, m[sl]) + n[sl]
    s_ref[...] = s

    inv_n = 1.0 / HEAD_DIM
    o = jnp.stack(outs, axis=0).reshape(n_items, c, gw)
    mu = segsum(o) * inv_n
    oc = o - mu
    var = segsum(oc * oc) * inv_n
    yn = oc * lax.rsqrt(var + GN_EPS) * per_group(lng_ref) + per_group(lnb_ref)
    bonus = segsum(r * k2 * per_group(rkp_ref)) * v
    y = (yn + bonus).reshape(nc, ng, c, gw)
    for g in range(ng):
        y_ref[0, :, g * gw:(g + 1) * gw] = y[:, g].reshape(tt, gw)

    @pl.when(t == nt - 1)
    def _():
        for g in range(ng):
            for h in range(WKV_GROUP_HEADS):
                hs = slice(h * HEAD_DIM, (h + 1) * HEAD_DIM)
                sout_ref[0, g * WKV_GROUP_HEADS + h] = s_ref[g, hs, hs]


def _wkv_chunked(r, lw, k, v, a, kkp, kap, rkp, lng, lnb):
    bsz, t, d = r.shape
    tt = min(t, ROW_TILE)
    assert t % tt == 0 and tt % WKV_CHUNK == 0 and d % WKV_GROUP == 0
    n_heads = d // HEAD_DIM
    gw = WKV_GROUP
    li = lax.broadcasted_iota(jnp.int32, (gw, gw), 0) // HEAD_DIM
    lj = lax.broadcasted_iota(jnp.int32, (gw, gw), 1) // HEAD_DIM
    ones_bd = (li == lj).astype(BF16)
    ti = lax.broadcasted_iota(jnp.int32, (WKV_CHUNK, WKV_CHUNK), 0)
    tj = lax.broadcasted_iota(jnp.int32, (WKV_CHUNK, WKV_CHUNK), 1)
    tri = (ti >= tj).astype(BF16)
    tile = pl.BlockSpec((1, tt, d), lambda b, i: (b, i, 0))
    par = pl.BlockSpec((1, d), lambda b, i: (0, 0))
    return pl.pallas_call(
        functools.partial(_wkv_chunk_kernel, tt, d),
        out_shape=[jax.ShapeDtypeStruct((bsz, t, d), F32),
                   jax.ShapeDtypeStruct((bsz, n_heads, HEAD_DIM, HEAD_DIM), F32)],
        grid=(bsz, t // tt),
        in_specs=[tile] * 5 + [par] * 5 + [_full_spec(ones_bd), _full_spec(tri)],
        out_specs=[tile, pl.BlockSpec((1, n_heads, HEAD_DIM, HEAD_DIM), lambda b, i: (b, 0, 0, 0))],
        scratch_shapes=[pltpu.VMEM((d // gw, gw, gw), F32)],
        compiler_params=_params(("arbitrary", "arbitrary")),
        name="wkv_chunked",
    )(r, lw, k, v, a, kkp, kap, rkp, lng, lnb, ones_bd, tri)


def _mixer_post_kernel(has_res, bb, tm, d, *refs):
    if has_res:
        xa_ref, xb_ref, gtp_ref, *refs = refs
    else:
        xa_ref, *refs = refs
    (yag_ref, gb_ref, y_ref, gt1_ref, sh_ref, sc_ref, g_ref, wout_ref, wrh_ref, wrl_ref, br_ref,
     x1_ref, h2_ref, comb_ref) = refs
    rows = bb * tm
    x = xa_ref[...].reshape(rows, d)
    if has_res:
        x = x + _per_seq(gtp_ref, bb, tm, d) * xb_ref[...].reshape(rows, d)
    merged = (yag_ref[...] + gb_ref[...] * y_ref[...]).reshape(rows, d)
    x1 = x + _per_seq(gt1_ref, bb, tm, d) * _dot(merged.astype(BF16), wout_ref[...])
    x1_ref[...] = x1.reshape(bb, tm, d)
    h2 = _rmsnorm(x1, g_ref[...]) * (1.0 + _per_seq(sc_ref, bb, tm, d)) + _per_seq(sh_ref, bb, tm, d)
    h_hi = h2.astype(BF16)
    h2_ref[...] = h_hi.reshape(bb, tm, d)
    h_lo = (h2 - h_hi.astype(F32)).astype(BF16)
    logits = _dot(h_hi, wrh_ref[...]) + (_dot(h_lo, wrh_ref[...]) + _dot(h_hi, wrl_ref[...])) + br_ref[...]

    lane_i = lax.broadcasted_iota(jnp.int32, logits.shape, 1)
    lane = lane_i.astype(F32)
    grp_of_lane = lax.shift_right_logical(lane_i, 3).astype(F32)
    neg = -jnp.inf
    none = 1e9
    is_g = (lane_i >= N_EXPERTS) & (lane_i < N_EXPERTS + N_GROUPS)
    glv = jnp.where(is_g, logits, neg)
    gmax = jnp.max(glv, axis=-1, keepdims=True)
    gidx = jnp.min(jnp.where(is_g & (glv == gmax), lane - N_EXPERTS, none), axis=-1, keepdims=True)
    pg = 1.0 / jnp.sum(jnp.where(is_g, jnp.exp(glv - gmax), 0.0), axis=-1, keepdims=True)
    in_grp = (lane_i < N_EXPERTS) & (grp_of_lane == gidx)
    ev = jnp.where(in_grp, logits, neg)
    v1 = jnp.max(ev, axis=-1, keepdims=True)
    i1 = jnp.min(jnp.where(in_grp & (ev == v1), lane, none), axis=-1, keepdims=True)
    ev2 = jnp.where(lane == i1, neg, ev)
    v2 = jnp.max(ev2, axis=-1, keepdims=True)
    i2 = jnp.min(jnp.where(in_grp & (ev2 == v2) & (lane != i1), lane, none), axis=-1, keepdims=True)
    e2 = jnp.exp(v2 - v1)
    den = 1.0 + e2
    comb = jnp.where(lane == i1, (1.0 / den) * pg, 0.0) + jnp.where(lane == i2, (e2 / den) * pg, 0.0)
    comb_ref[...] = comb.reshape(bb, tm, ROUTER_LANES)


def _mixer_post(xa, res, yag, gb, y, gt1, sh, sc, g, wout_bf, wr_hi, wr_lo, br):
    bsz, t, d = xa.shape
    bb, tm = _row_tiling(bsz, t)
    tile = pl.BlockSpec((bb, tm, d), lambda b, i: (b, i, 0))
    perb = pl.BlockSpec((bb, 1, d), lambda b, i: (b, 0, 0))
    ins, specs = [xa], [tile]
    if res is not None:
        ins += [res[0], res[1]]
        specs += [tile, perb]
    ins += [yag, gb, y, gt1, sh, sc, g, wout_bf, wr_hi, wr_lo, br]
    specs += [tile, tile, tile, perb, perb, perb] + [_full_spec(p) for p in (g, wout_bf, wr_hi, wr_lo, br)]
    return pl.pallas_call(
        functools.partial(_mixer_post_kernel, res is not None, bb, tm, d),
        out_shape=[jax.ShapeDtypeStruct((bsz, t, d), F32), jax.ShapeDtypeStruct((bsz, t, d), BF16),
                   jax.ShapeDtypeStruct((bsz, t, ROUTER_LANES), F32)],
        grid=(bsz // bb, t // tm),
        in_specs=specs,
        out_specs=[tile, tile, pl.BlockSpec((bb, tm, ROUTER_LANES), lambda b, i: (b, i, 0))],
        compiler_params=_params(("arbitrary", "arbitrary")),
        name="mixer_post",
    )(*ins)


def _moe_kernel(h_ref, comb_ref, wg_ref, wu_ref, wd_ref, o_ref):
    e = pl.program_id(1)

    @pl.when(e == 0)
    def _():
        o_ref[...] = jnp.zeros_like(o_ref)

    hb = h_ref[...]
    hg = _dot(hb, wg_ref[0])
    hu = _dot(hb, wu_ref[0])
    comb = comb_ref[...]
    lane = lax.broadcasted_iota(jnp.int32, comb.shape, 1)
    ce = jnp.sum(jnp.where(lane == e, comb, 0.0), axis=-1, keepdims=True)
    act = hg * _sigmoid(hg) * hu * ce
    o_ref[...] += _dot(act.astype(BF16), wd_ref[0])


def _moe(h2, comb, wg_bf, wu_bf, wd_bf):
    n, d = h2.shape
    ne, _, f = wg_bf.shape
    tm = min(n, 1024)
    return pl.pallas_call(
        _moe_kernel,
        out_shape=jax.ShapeDtypeStruct((n, d), F32),
        grid=(n // tm, ne),
        in_specs=[
            pl.BlockSpec((tm, d), lambda i, e: (i, 0)),
            pl.BlockSpec((tm, ROUTER_LANES), lambda i, e: (i, 0)),
            pl.BlockSpec((1, d, f), lambda i, e: (e, 0, 0)),
            pl.BlockSpec((1, d, f), lambda i, e: (e, 0, 0)),
            pl.BlockSpec((1, f, d), lambda i, e: (e, 0, 0)),
        ],
        out_specs=pl.BlockSpec((tm, d), lambda i, e: (i, 0)),
        compiler_params=_params(("arbitrary", "arbitrary")),
        name="moe_experts",
    )(h2, comb, wg_bf, wu_bf, wd_bf)


def _final_kernel(bb, tm, d, xa_ref, xb_ref, gtp_ref, g_ref, o_ref):
    rows = bb * tm
    x = xa_ref[...].reshape(rows, d) + _per_seq(gtp_ref, bb, tm, d) * xb_ref[...].reshape(rows, d)
    o_ref[...] = _rmsnorm(x, g_ref[...]).reshape(bb, tm, d)


def _final_norm(xa, xb, gtp, g):
    bsz, t, d = xa.shape
    bb, tm = _row_tiling(bsz, t)
    tile = pl.BlockSpec((bb, tm, d), lambda b, i: (b, i, 0))
    return pl.pallas_call(
        functools.partial(_final_kernel, bb, tm, d),
        out_shape=jax.ShapeDtypeStruct((bsz, t, d), F32),
        grid=(bsz // bb, t // tm),
        in_specs=[tile, tile, pl.BlockSpec((bb, 1, d), lambda b, i: (b, 0, 0)), _full_spec(g)],
        out_specs=tile,
        compiler_params=_params(("arbitrary", "arbitrary")),
        name="final_norm",
    )(xa, xb, gtp, g)


def _to_lanes(z, n_heads):
    bsz, t, _ = z.shape
    return z.reshape(bsz, t, n_heads, HEAD_DIM).transpose(1, 3, 0, 2).reshape(t, HEAD_DIM, bsz * n_heads)


def _from_lanes(z, bsz, n_heads):
    t = z.shape[0]
    return z.reshape(t, HEAD_DIM, bsz, n_heads).transpose(2, 0, 3, 1).reshape(bsz, t, n_heads * HEAD_DIM)


def _head_param(p, n_heads):
    return jnp.tile(p.reshape(n_heads, HEAD_DIM).T, (1, V7X_LANES // n_heads))


def _trunk(x, c, shift0, conv0, wkv0, W):
    bsz, t, d = x.shape
    n_heads = d // HEAD_DIM
    depth = W["w_in_bf"].shape[0]
    mod = _modulation(c, W["ada_w"], W["ada_b"])
    bufs, shifts, states = [], [], []
    res = None
    xa = x
    for l in range(depth):
        sh1, sc1, gt1, sh2, sc2, gt2 = (mod[l, :, i * d:(i + 1) * d][:, None, :] for i in range(6))
        row = lambda p: p[l][None, :]
        yag, gb, r, lw, k, v, a, convn, shiftn = _mixer_pre(
            xa, res, sh1, sc1, row(W["norm_mix_g"]), W["w_in_bf"][l], shift0[l][:, None, :], conv0[l],
            row(W["mu_shift"]), W["conv_w"][l], row(W["conv_b"]), row(W["conv_ln_g"]), row(W["conv_ln_b"]),
            row(W["rw_w0"]), W["lora_bf"][l], row(W["rw_a0"]), W["g_up_bf"][l])
        head_params = (W["rw_k_k"][l], W["rw_k_a"][l], W["rw_r_k"][l].reshape(-1), W["rw_lnx_g"][l], W["rw_lnx_b"][l])
        if wkv0 is None:
            y, s_l = _wkv_chunked(r, lw, k, v, a, *(p[None, :] for p in head_params))
            states.append(s_l)
        else:
            s0 = wkv0[l].transpose(2, 3, 0, 1).reshape(HEAD_DIM, HEAD_DIM, bsz * n_heads)
            y_l, s_l = _wkv_scan(*(_to_lanes(z, n_heads) for z in (r, lw, k, v, a)),
                                 *(_head_param(p, n_heads) for p in head_params), s0)
            y = _from_lanes(y_l, bsz, n_heads)
            states.append(s_l.reshape(HEAD_DIM, HEAD_DIM, bsz, n_heads).transpose(2, 3, 0, 1))
        x1, h2, comb = _mixer_post(xa, res, yag, gb, y, gt1, sh2, sc2, row(W["norm_ffn_g"]), W["w_out_bf"][l],
                                   W["wr_hi"][l], W["wr_lo"][l], W["br"][l])
        moe = _moe(h2.reshape(bsz * t, d), comb.reshape(bsz * t, ROUTER_LANES),
                   W["moe_wg_bf"][l], W["moe_wu_bf"][l], W["moe_wd_bf"][l]).reshape(bsz, t, d)
        xa, res = x1, (moe, gt2)
        bufs.append(convn)
        shifts.append(shiftn[:, 0, :])
    y_out = _final_norm(xa, res[0], res[1], W["final_norm_g"][None, :])
    return y_out, jnp.stack(bufs), jnp.stack(shifts), jnp.stack(states)


def kernel(x_prompt, x_sample, state_conv, state_shift, state_wkv, c_prompt, c_sample, norm_mix_g, norm_ffn_g, final_norm_g, ada_w, ada_b, w_in, mu_shift, conv_w, conv_b, conv_ln_g, conv_ln_b, rw_w0, rw_w_up, rw_a0, rw_a_up, rw_g_up, rw_k_k, rw_k_a, rw_r_k, rw_lnx_g, rw_lnx_b, w_out, moe_w_group, moe_b_group, moe_w_expert, moe_b_expert, moe_w_gate, moe_w_up, moe_w_down):
    depth, d, _ = w_in.shape
    zw = jnp.zeros_like(rw_w_up)
    lora = jnp.concatenate([jnp.concatenate([rw_w_up, zw], axis=2), jnp.concatenate([zw, rw_a_up], axis=2)], axis=1)
    pad = ROUTER_LANES - N_EXPERTS - N_GROUPS
    wr = jnp.concatenate([moe_w_expert, moe_w_group, jnp.zeros((depth, d, pad), F32)], axis=2)
    wr_hi = wr.astype(BF16)
    br = jnp.concatenate([moe_b_expert, moe_b_group, jnp.zeros((depth, pad), F32)], axis=1)[:, None, :]
    W = dict(norm_mix_g=norm_mix_g, norm_ffn_g=norm_ffn_g, final_norm_g=final_norm_g, ada_w=ada_w, ada_b=ada_b,
             w_in_bf=w_in.astype(BF16), mu_shift=mu_shift, conv_w=conv_w, conv_b=conv_b, conv_ln_g=conv_ln_g,
             conv_ln_b=conv_ln_b, rw_w0=rw_w0, lora_bf=lora.astype(BF16), rw_a0=rw_a0, g_up_bf=rw_g_up.astype(BF16),
             rw_k_k=rw_k_k, rw_k_a=rw_k_a, rw_r_k=rw_r_k, rw_lnx_g=rw_lnx_g, rw_lnx_b=rw_lnx_b,
             w_out_bf=w_out.astype(BF16), wr_hi=wr_hi, wr_lo=(wr - wr_hi.astype(F32)).astype(BF16), br=br,
             moe_wg_bf=moe_w_gate.astype(BF16), moe_wu_bf=moe_w_up.astype(BF16), moe_wd_bf=moe_w_down.astype(BF16))
    bp = x_prompt.shape[0]
    conv0 = jnp.zeros((depth, bp, CONV_BUF, d), x_prompt.dtype)
    shift0 = jnp.zeros((depth, bp, d), x_prompt.dtype)
    y_p, conv_p, shift_p, wkv_p = _trunk(x_prompt, c_prompt, shift0, conv0, None, W)
    y_s, conv_s, shift_s, wkv_s = _trunk(x_sample, c_sample, state_shift, state_conv, state_wkv, W)
    return (y_p, y_s, conv_p, shift_p, wkv_p.astype(state_wkv.dtype), conv_s, shift_s, wkv_s.astype(state_wkv.dtype))
```

```python
import functools

import jax
import jax.numpy as jnp
from jax import lax
from jax.experimental import pallas as pl
from jax.experimental.pallas import tpu as pltpu

F32 = jnp.float32
BF16 = jnp.bfloat16

HEAD_DIM = 64
CONV_WIDTH = 31
CONV_BUF = CONV_WIDTH - 1
LORA_W = 64
LORA_A = 64
N_GROUPS = 4
EXPERTS_PER_GROUP = 8
N_EXPERTS = N_GROUPS * EXPERTS_PER_GROUP
RMS_EPS = 1e-6
LN_EPS = 1e-5
GN_EPS = 64e-5

V7X_LANES = 128
V7X_SUBLANES = 8
V7X_VMEM_LIMIT_BYTES = 56 * 1024 * 1024

ROW_TILE = 256
SHORT_SEQ_ROWS = 128
HIST_ROWS = 32
HIST_PAD = HIST_ROWS - CONV_BUF
CONV_LANE_CHUNK = 256
WKV_CHUNK = 64
WKV_GROUP_HEADS = 4
WKV_GROUP = WKV_GROUP_HEADS * HEAD_DIM
ROUTER_LANES = V7X_LANES
ROUTE_E1, ROUTE_E2, ROUTE_W1, ROUTE_W2 = 0, 1, 2, 3
MOE_ROW_TILE = 256
MOE_ROW_TILE_SMALL = 64


def _dot(a, b):
    return jnp.dot(a, b, preferred_element_type=F32)


def _sigmoid(x):
    return 1.0 / (1.0 + jnp.exp(-x))


def _rmsnorm(x, g):
    return x * lax.rsqrt(jnp.mean(x * x, axis=-1, keepdims=True) + RMS_EPS) * g


def _params(sem):
    return pltpu.CompilerParams(dimension_semantics=sem, vmem_limit_bytes=V7X_VMEM_LIMIT_BYTES)


def _row_tiling(bsz, t):
    tm = min(t, ROW_TILE)
    bb = max(1, min(bsz, SHORT_SEQ_ROWS // tm))
    assert t % tm == 0 and bsz % bb == 0 and tm % V7X_SUBLANES == 0
    assert bb == 1 or tm == t
    return bb, tm


def _per_seq(ref, bb, n, d):
    p = ref[...]
    if bb == 1:
        return p[0]
    return jnp.broadcast_to(p, (bb, n, d)).reshape(bb * n, d)


def _full_spec(a):
    return pl.BlockSpec(a.shape, lambda b, i: (0,) * a.ndim)


def _moe_residual(x, y0_ref, y1_ref, route_ref, gate):
    route = route_ref[...].reshape(x.shape[0], ROUTER_LANES)
    w1 = route[:, ROUTE_W1:ROUTE_W1 + 1]
    w2 = route[:, ROUTE_W2:ROUTE_W2 + 1]
    return x + gate * (w1 * y0_ref[...] + w2 * y1_ref[...])


def _moe_residual_operands(res, bsz, t, bb, tm):
    ybuf, route, gate = res
    d = ybuf.shape[1]
    rows = bb * tm
    n_tiles = t // tm
    slot_stride = bsz * t // rows
    return ([ybuf, ybuf, route, gate],
            [pl.BlockSpec((rows, d), lambda b, i: (b * n_tiles + i, 0)),
             pl.BlockSpec((rows, d), lambda b, i: (slot_stride + b * n_tiles + i, 0)),
             pl.BlockSpec((bb, tm, ROUTER_LANES), lambda b, i: (b, i, 0)),
             pl.BlockSpec((bb, 1, d), lambda b, i: (b, 0, 0))])


def _ada_kernel(c_ref, w_ref, b_ref, o_ref):
    c = c_ref[...]
    s = c * _sigmoid(c)
    o_ref[0] = _dot(s.astype(BF16), w_ref[0].astype(BF16)) + b_ref[0]


def _modulation(c, ada_w, ada_b):
    depth, d, n = ada_w.shape
    bsz = c.shape[0]
    tn = 1536
    return pl.pallas_call(
        _ada_kernel,
        out_shape=jax.ShapeDtypeStruct((depth, bsz, n), F32),
        grid=(depth, n // tn),
        in_specs=[
            pl.BlockSpec((bsz, d), lambda l, j: (0, 0)),
            pl.BlockSpec((1, d, tn), lambda l, j: (l, 0, j)),
            pl.BlockSpec((1, 1, tn), lambda l, j: (l, 0, j)),
        ],
        out_specs=pl.BlockSpec((1, bsz, tn), lambda l, j: (l, 0, j)),
        compiler_params=_params(("arbitrary", "arbitrary")),
        name="ada_modulation",
    )(c, ada_w, ada_b.reshape(depth, 1, n))


def _mixer_pre_kernel(has_res, bb, tm, d, *refs):
    if has_res:
        xa_ref, y0_ref, y1_ref, route_ref, gtp_ref, *refs = refs
    else:
        xa_ref, *refs = refs
    (sh_ref, sc_ref, g_ref, win_ref, shift0_ref, conv0_ref, mu_ref, cw_ref, cb_ref, clg_ref, clb_ref,
     w0_ref, lora_ref, a0_ref, gup_ref, *refs) = refs
    if has_res:
        xres_ref, *refs = refs
    (yag_ref, gb_ref, r_ref, lw_ref, k_ref, v_ref, a_ref, convn_ref, shiftn_ref,
     ubuf_ref, z_ref, carry_ref, shifted_ref) = refs
    c_plain = 4 * d
    rows = bb * tm
    t = pl.program_id(1)
    nt = pl.num_programs(1)

    x = xa_ref[...].reshape(rows, d)
    if has_res:
        x = _moe_residual(x, y0_ref, y1_ref, route_ref, _per_seq(gtp_ref, bb, tm, d))
        xres_ref[...] = x.reshape(bb, tm, d)
    h = _rmsnorm(x, g_ref[...]) * (1.0 + _per_seq(sc_ref, bb, tm, d)) + _per_seq(sh_ref, bb, tm, d)
    proj = _dot(h.astype(BF16), win_ref[...])

    @pl.when(t == 0)
    def _():
        hp = _per_seq(shift0_ref, bb, V7X_SUBLANES, d)
        hp = jnp.broadcast_to(hp, (bb * V7X_SUBLANES, d)).astype(BF16)
        carry_ref[...] = _dot(hp, win_ref[:, c_plain:])
        ubuf_ref[:, pl.ds(0, HIST_PAD), :] = jnp.zeros((bb, HIST_PAD, d), F32)
        ubuf_ref[:, pl.ds(HIST_PAD, CONV_BUF), :] = conv0_ref[...]

    @pl.when(t == nt - 1)
    def _():
        shiftn_ref[...] = h.reshape(bb, tm, d)[:, tm - 1:tm, :]

    cur = proj[:, c_plain:]
    prev = pltpu.roll(cur, 1, 0)
    row = lax.broadcasted_iota(jnp.int32, cur.shape, 0)
    if bb == 1:
        prev = jnp.where(row == 0, carry_ref[pl.ds(0, 1), :], prev)
        carry_ref[pl.ds(0, 1), :] = cur[tm - 1:tm, :]
    else:
        prev = jnp.where((row & (tm - 1)) == 0, carry_ref[...], prev)
    rw = cur + (prev - cur) * mu_ref[...]

    r = rw[:, 0:d]
    k = rw[:, d:2 * d]
    v = rw[:, 2 * d:3 * d]
    xwa = rw[:, 3 * d:3 * d + LORA_W + LORA_A]
    xg = rw[:, 3 * d + LORA_W + LORA_A:]
    lane = lax.broadcasted_iota(jnp.int32, xwa.shape, 1)
    lhs = jnp.where(lane < LORA_W, jnp.tanh(xwa), xwa)
    za = _dot(lhs.astype(BF16), lora_ref[...])
    log_decay = -jnp.exp(-0.5) * _sigmoid(w0_ref[...] + za[:, 0:d])
    a = _sigmoid(a0_ref[...] + za[:, d:])
    gate = _dot(_sigmoid(xg).astype(BF16), gup_ref[...])
    blk = (bb, tm, d)
    r_ref[...] = r.reshape(blk)
    k_ref[...] = k.reshape(blk)
    v_ref[...] = v.reshape(blk)
    lw_ref[...] = log_decay.reshape(blk)
    a_ref[...] = a.reshape(blk)
    gb_ref[...] = (_sigmoid(proj[:, 3 * d:4 * d]) * gate).reshape(blk)

    u = proj[:, 0:d] * _sigmoid(proj[:, d:2 * d])
    ubuf_ref[:, pl.ds(HIST_ROWS, tm), :] = u.reshape(blk)
    rb = min(tm, 32)
    lc = CONV_LANE_CHUNK
    span = tm + HIST_ROWS - V7X_SUBLANES
    for b in range(bb):
        for c0 in range(0, d, lc):
            for s in range(1, V7X_SUBLANES):
                shifted_ref[s - 1] = ubuf_ref[b, pl.ds(s, span), c0:c0 + lc]
            for r0 in range(0, tm, rb):
                acc = jnp.broadcast_to(cb_ref[:, c0:c0 + lc], (rb, lc))
                for tap in range(CONV_WIDTH):
                    s = (HIST_PAD + tap) % V7X_SUBLANES
                    base = r0 + HIST_PAD + tap - s
                    if s == 0:
                        src = ubuf_ref[b, pl.ds(base, rb), c0:c0 + lc]
                    else:
                        src = shifted_ref[s - 1, pl.ds(base, rb), :]
                    acc = acc + cw_ref[pl.ds(tap, 1), c0:c0 + lc] * src
                z_ref[pl.ds(b * tm + r0, rb), c0:c0 + lc] = acc
    z = z_ref[...]
    mu = jnp.mean(z, axis=-1, keepdims=True)
    zc = z - mu
    var = jnp.mean(zc * zc, axis=-1, keepdims=True)
    zn = zc * lax.rsqrt(var + LN_EPS) * clg_ref[...] + clb_ref[...]
    yag_ref[...] = (_sigmoid(proj[:, 2 * d:3 * d]) * (zn * _sigmoid(zn))).reshape(blk)

    @pl.when(t == nt - 1)
    def _():
        convn_ref[...] = ubuf_ref[:, pl.ds(tm + HIST_PAD, CONV_BUF), :]

    if tm >= HIST_ROWS:
        @pl.when(t < nt - 1)
        def _():
            ubuf_ref[:, pl.ds(0, HIST_ROWS), :] = ubuf_ref[:, pl.ds(tm, HIST_ROWS), :]


def _mixer_pre(xa, res, sh, sc, g, win_bf, shift0, conv0, mu, cw, cb, clg, clb, w0, lora_bf, a0, gup_bf):
    bsz, t, d = xa.shape
    bb, tm = _row_tiling(bsz, t)
    assert t == tm or tm >= HIST_ROWS
    assert bb == 1 or tm == V7X_SUBLANES
    c_rwkv = win_bf.shape[1] - 4 * d
    tile = pl.BlockSpec((bb, tm, d), lambda b, i: (b, i, 0))
    perb = pl.BlockSpec((bb, 1, d), lambda b, i: (b, 0, 0))
    hist = pl.BlockSpec((bb, CONV_BUF, d), lambda b, i: (b, 0, 0))
    ins, specs = [xa], [tile]
    n_act = 7
    if res is not None:
        res_ins, res_specs = _moe_residual_operands(res, bsz, t, bb, tm)
        ins += res_ins
        specs += res_specs
        n_act += 1
    ins += [sh, sc, g, win_bf, shift0, conv0, mu, cw, cb, clg, clb, w0, lora_bf, a0, gup_bf]
    specs += [perb, perb, _full_spec(g),
              pl.BlockSpec(win_bf.shape, lambda b, i: (0, 0), pipeline_mode=pl.Buffered(1)),
              perb, hist] + [_full_spec(p) for p in (mu, cw, cb, clg, clb, w0, lora_bf, a0, gup_bf)]
    act = jax.ShapeDtypeStruct((bsz, t, d), F32)
    return pl.pallas_call(
        functools.partial(_mixer_pre_kernel, res is not None, bb, tm, d),
        out_shape=[act] * n_act + [jax.ShapeDtypeStruct((bsz, CONV_BUF, d), F32),
                                   jax.ShapeDtypeStruct((bsz, 1, d), F32)],
        grid=(bsz // bb, t // tm),
        in_specs=specs,
        out_specs=[tile] * n_act + [hist, perb],
        scratch_shapes=[pltpu.VMEM((bb, tm + HIST_ROWS, d), F32), pltpu.VMEM((bb * tm, d), F32),
                        pltpu.VMEM((bb * V7X_SUBLANES, c_rwkv), F32),
                        pltpu.VMEM((V7X_SUBLANES - 1, tm + HIST_ROWS - V7X_SUBLANES, CONV_LANE_CHUNK), F32)],
        compiler_params=_params(("arbitrary", "arbitrary")),
        name="mixer_pre",
    )(*ins)


def _wkv_kernel(tc, r_ref, lw_ref, k_ref, v_ref, a_ref, kkp_ref, kap_ref, rkp_ref, lng_ref, lnb_ref, s0_ref,
                y_ref, s_ref, o_ref):
    n = HEAD_DIM

    @pl.when(pl.program_id(1) == 0)
    def _():
        s_ref[...] = s0_ref[...]

    def step(tt, carry):
        r_t = r_ref[tt]
        k_t = k_ref[tt]
        a_t = a_ref[tt]
        w_t = jnp.exp(lw_ref[tt])
        kk = k_t * kkp_ref[...]
        kk = kk * lax.rsqrt(jnp.sum(kk * kk, axis=0, keepdims=True) + 1e-12)
        b_t = kk * a_t
        k2 = k_t * (1.0 + (a_t - 1.0) * kap_ref[...])

        def row(i, c):
            si = s_ref[i]
            sa = jnp.sum(si * kk, axis=0, keepdims=True)
            vi = v_ref[tt, pl.ds(i, 1), :]
            sn = si * w_t - sa * b_t + vi * k2
            s_ref[i] = sn
            o_ref[pl.ds(i, 1), :] = jnp.sum(sn * r_t, axis=0, keepdims=True)
            return c

        lax.fori_loop(0, n, row, 0, unroll=8)
        o = o_ref[...]
        mu = jnp.mean(o, axis=0, keepdims=True)
        oc = o - mu
        var = jnp.mean(oc * oc, axis=0, keepdims=True)
        yn = oc * lax.rsqrt(var + GN_EPS) * lng_ref[...] + lnb_ref[...]
        bonus = jnp.sum(r_t * k2 * rkp_ref[...], axis=0, keepdims=True) * v_ref[tt]
        y_ref[tt] = yn + bonus
        return carry

    lax.fori_loop(0, tc, step, 0)


def _wkv_scan(r, w, k, v, a, kkp, kap, rkp, lng, lnb, s0):
    t, n, nl = r.shape
    tc = min(t, 32)
    lb = V7X_LANES
    seq = pl.BlockSpec((tc, n, lb), lambda l, i: (i, 0, l))
    par = pl.BlockSpec((n, lb), lambda l, i: (0, 0))
    st = pl.BlockSpec((n, n, lb), lambda l, i: (0, 0, l))
    return pl.pallas_call(
        functools.partial(_wkv_kernel, tc),
        out_shape=[jax.ShapeDtypeStruct((t, n, nl), F32), jax.ShapeDtypeStruct((n, n, nl), F32)],
        grid=(nl // lb, t // tc),
        in_specs=[seq] * 5 + [par] * 5 + [st],
        out_specs=[seq, st],
        scratch_shapes=[pltpu.VMEM((n, lb), F32)],
        compiler_params=_params(("arbitrary", "arbitrary")),
        name="wkv_scan",
    )(r, w, k, v, a, kkp, kap, rkp, lng, lnb, s0)


def _bmm(a, b):
    return jnp.einsum("gck,gkn->gcn", a, b, preferred_element_type=F32)


def _bmm_nt(a, b):
    return jnp.einsum("gck,gnk->gcn", a, b, preferred_element_type=F32)


def _bmm_tn(a, b):
    return jnp.einsum("gtc,gtn->gcn", a, b, preferred_element_type=F32)


def _split_bf16(x):
    hi = x.astype(BF16)
    return hi, (x - hi.astype(F32)).astype(BF16)


def _wkv_chunk_kernel(tt, d, r_ref, lw_ref, k_ref, v_ref, a_ref, kkp_ref, kap_ref, rkp_ref, lng_ref, lnb_ref,
                      ones_ref, tri_ref, y_ref, sout_ref, s_ref):
    c = WKV_CHUNK
    gw = WKV_GROUP
    ng = d // gw
    nc = tt // c
    n_items = nc * ng
    t = pl.program_id(1)
    nt = pl.num_programs(1)

    @pl.when(t == 0)
    def _():
        s_ref[...] = jnp.zeros_like(s_ref)

    lane = lax.broadcasted_iota(jnp.int32, (1, c, gw), 2)
    rowi = lax.broadcasted_iota(jnp.int32, (1, c, gw), 1)
    head_of_lane = lax.shift_right_logical(lane, 6)
    hmask = [head_of_lane == h for h in range(WKV_GROUP_HEADS)]
    tcol = lane & (c - 1)
    strict = tcol < rowi
    incl = tcol <= rowi
    bl = lax.broadcasted_iota(jnp.int32, (1, gw, gw), 2)
    br = lax.broadcasted_iota(jnp.int32, (1, gw, gw), 1)
    blockmask = lax.shift_right_logical(bl, 6) == lax.shift_right_logical(br, 6)

    def items(ref):
        x = ref[0].reshape(nc, c, d)
        return jnp.stack([x[:, :, g * gw:(g + 1) * gw] for g in range(ng)], axis=1).reshape(n_items, c, gw)

    def per_group(ref):
        p = jnp.stack([ref[:, g * gw:(g + 1) * gw] for g in range(ng)], axis=0)
        return jnp.broadcast_to(p[None], (nc, ng, 1, gw)).reshape(n_items, 1, gw)

    def bd(xf):
        return jnp.concatenate([jnp.where(hmask[h], xf, 0.0) for h in range(WKV_GROUP_HEADS)], axis=1).astype(BF16)

    def segsum(x):
        shp = x.shape
        hi, lo = _split_bf16(x.reshape(-1, gw))
        return (_dot(hi, ones_ref[...]) + _dot(lo, ones_ref[...])).reshape(shp)

    r, lw, k, v, a = (items(ref) for ref in (r_ref, lw_ref, k_ref, v_ref, a_ref))
    kk = k * per_group(kkp_ref)
    kk = kk * lax.rsqrt(segsum(kk * kk) + 1e-12)
    b = kk * a
    k2 = k * (1.0 + (a - 1.0) * per_group(kap_ref))
    lw_hi, lw_lo = _split_bf16(lw)
    tri = jnp.broadcast_to(tri_ref[...][None], (n_items, c, c))
    cum = _bmm(tri, lw_hi) + _bmm(tri, lw_lo)
    cl = cum[:, c - 1:c, :]
    at = -(kk * jnp.exp(cum - lw))
    rt = r * jnp.exp(cum)
    winv = jnp.exp(-cum)
    wrem = jnp.exp(cl - cum)
    wc = jnp.exp(cl)
    lhs = jnp.concatenate([at, rt], axis=1).astype(BF16)
    rhs = jnp.concatenate([bd(b * winv), bd(k2 * winv)], axis=1)
    sc = _bmm_nt(lhs, rhs)
    l_ab = jnp.where(strict, sc[:, 0:c, 0:gw], 0.0)
    l_ak = jnp.where(strict, sc[:, 0:c, gw:], 0.0)
    l_rb = jnp.where(incl, sc[:, c:, 0:gw], 0.0)
    l_rk = jnp.where(incl, sc[:, c:, gw:], 0.0)
    pk = l_ab
    tm = l_ab
    pbd = bd(pk)
    for _ in range(c.bit_length() - 2):
        pk = _bmm(pk.astype(BF16), pbd)
        pbd = bd(pk)
        tm = tm + pk + _bmm(tm.astype(BF16), pbd)
    tmb = tm.astype(BF16)
    vbd = bd(v)
    ah = at + _bmm(tmb, bd(at))
    x1 = _bmm(l_ak.astype(BF16), vbd)
    gm = x1 + _bmm(tmb, bd(x1))
    l_rbb = l_rb.astype(BF16)
    rh = (rt + _bmm(l_rbb, bd(ah))).astype(BF16)
    o_ind = _bmm(jnp.concatenate([l_rbb, l_rk.astype(BF16)], axis=2), jnp.concatenate([bd(gm), vbd], axis=1))
    bb = (b * wrem).astype(BF16)
    kb = (k2 * wrem).astype(BF16)
    m = jnp.where(blockmask, _bmm_tn(ah.astype(BF16), bb), 0.0).astype(BF16)
    n = jnp.where(blockmask, _bmm_tn(jnp.concatenate([gm, v], axis=1).astype(BF16),
                                     jnp.concatenate([bb, kb], axis=1)), 0.0)

    s = s_ref[...]
    outs = []
    for ci in range(nc):
        sl = slice(ci * ng, (ci + 1) * ng)
        sb = s.astype(BF16)
        outs.append(_bmm_nt(rh[sl], sb) + o_ind[sl])
        s = s * wc[sl] + _bmm(sb, m[sl]) + n[sl]
    s_ref[...] = s

    inv_n = 1.0 / HEAD_DIM
    o = jnp.stack(outs, axis=0).reshape(n_items, c, gw)
    mu = segsum(o) * inv_n
    oc = o - mu
    var = segsum(oc * oc) * inv_n
    yn = oc * lax.rsqrt(var + GN_EPS) * per_group(lng_ref) + per_group(lnb_ref)
    bonus = segsum(r * k2 * per_group(rkp_ref)) * v
    y = (yn + bonus).reshape(nc, ng, c, gw)
    for g in range(ng):
        y_ref[0, :, g * gw:(g + 1) * gw] = y[:, g].reshape(tt, gw)

    @pl.when(t == nt - 1)
    def _():
        for g in range(ng):
            for h in range(WKV_GROUP_HEADS):
                hs = slice(h * HEAD_DIM, (h + 1) * HEAD_DIM)
                sout_ref[0, g * WKV_GROUP_HEADS + h] = s_ref[g, hs, hs]


def _wkv_chunked(r, lw, k, v, a, kkp, kap, rkp, lng, lnb):
    bsz, t, d = r.shape
    tt = min(t, ROW_TILE)
    assert t % tt == 0 and tt % WKV_CHUNK == 0 and d % WKV_GROUP == 0
    n_heads = d // HEAD_DIM
    gw = WKV_GROUP
    li = lax.broadcasted_iota(jnp.int32, (gw, gw), 0) // HEAD_DIM
    lj = lax.broadcasted_iota(jnp.int32, (gw, gw), 1) // HEAD_DIM
    ones_bd = (li == lj).astype(BF16)
    ti = lax.broadcasted_iota(jnp.int32, (WKV_CHUNK, WKV_CHUNK), 0)
    tj = lax.broadcasted_iota(jnp.int32, (WKV_CHUNK, WKV_CHUNK), 1)
    tri = (ti >= tj).astype(BF16)
    tile = pl.BlockSpec((1, tt, d), lambda b, i: (b, i, 0))
    par = pl.BlockSpec((1, d), lambda b, i: (0, 0))
    return pl.pallas_call(
        functools.partial(_wkv_chunk_kernel, tt, d),
        out_shape=[jax.ShapeDtypeStruct((bsz, t, d), F32),
                   jax.ShapeDtypeStruct((bsz, n_heads, HEAD_DIM, HEAD_DIM), F32)],
        grid=(bsz, t // tt),
        in_specs=[tile] * 5 + [par] * 5 + [_full_spec(ones_bd), _full_spec(tri)],
        out_specs=[tile, pl.BlockSpec((1, n_heads, HEAD_DIM, HEAD_DIM), lambda b, i: (b, 0, 0, 0))],
        scratch_shapes=[pltpu.VMEM((d // gw, gw, gw), F32)],
        compiler_params=_params(("arbitrary", "arbitrary")),
        name="wkv_chunked",
    )(r, lw, k, v, a, kkp, kap, rkp, lng, lnb, ones_bd, tri)


def _mixer_post_kernel(bb, tm, d, x_ref, yag_ref, gb_ref, y_ref, gt1_ref, sh_ref, sc_ref, g_ref, wout_ref,
                       wrh_ref, wrl_ref, br_ref, x1_ref, h2_ref, route_ref):
    rows = bb * tm
    x = x_ref[...].reshape(rows, d)
    merged = (yag_ref[...] + gb_ref[...] * y_ref[...]).reshape(rows, d)
    x1 = x + _per_seq(gt1_ref, bb, tm, d) * _dot(merged.astype(BF16), wout_ref[...])
    x1_ref[...] = x1.reshape(bb, tm, d)
    h2 = _rmsnorm(x1, g_ref[...]) * (1.0 + _per_seq(sc_ref, bb, tm, d)) + _per_seq(sh_ref, bb, tm, d)
    h2_ref[...] = h2.reshape(bb, tm, d)
    h_hi = h2.astype(BF16)
    h_lo = (h2 - h_hi.astype(F32)).astype(BF16)
    logits = _dot(h_hi, wrh_ref[...]) + (_dot(h_lo, wrh_ref[...]) + _dot(h_hi, wrl_ref[...])) + br_ref[...]

    lane_i = lax.broadcasted_iota(jnp.int32, logits.shape, 1)
    lane = lane_i.astype(F32)
    grp_of_lane = lax.shift_right_logical(lane_i, 3).astype(F32)
    neg = -jnp.inf
    none = 1e9
    is_g = (lane_i >= N_EXPERTS) & (lane_i < N_EXPERTS + N_GROUPS)
    glv = jnp.where(is_g, logits, neg)
    gmax = jnp.max(glv, axis=-1, keepdims=True)
    gidx = jnp.min(jnp.where(is_g & (glv == gmax), lane - N_EXPERTS, none), axis=-1, keepdims=True)
    pg = 1.0 / jnp.sum(jnp.where(is_g, jnp.exp(glv - gmax), 0.0), axis=-1, keepdims=True)
    in_grp = (lane_i < N_EXPERTS) & (grp_of_lane == gidx)
    ev = jnp.where(in_grp, logits, neg)
    v1 = jnp.max(ev, axis=-1, keepdims=True)
    i1 = jnp.min(jnp.where(in_grp & (ev == v1), lane, none), axis=-1, keepdims=True)
    ev2 = jnp.where(lane == i1, neg, ev)
    v2 = jnp.max(ev2, axis=-1, keepdims=True)
    i2 = jnp.min(jnp.where(in_grp & (ev2 == v2) & (lane != i1), lane, none), axis=-1, keepdims=True)
    e2 = jnp.exp(v2 - v1)
    den = 1.0 + e2
    route = (jnp.where(lane_i == ROUTE_E1, i1, 0.0) + jnp.where(lane_i == ROUTE_E2, i2, 0.0)
             + jnp.where(lane_i == ROUTE_W1, (1.0 / den) * pg, 0.0) + jnp.where(lane_i == ROUTE_W2, (e2 / den) * pg, 0.0))
    route_ref[...] = route.reshape(bb, tm, ROUTER_LANES)


def _mixer_post(x, yag, gb, y, gt1, sh, sc, g, wout_bf, wr_hi, wr_lo, br):
    bsz, t, d = x.shape
    bb, tm = _row_tiling(bsz, t)
    tile = pl.BlockSpec((bb, tm, d), lambda b, i: (b, i, 0))
    perb = pl.BlockSpec((bb, 1, d), lambda b, i: (b, 0, 0))
    ins = [x, yag, gb, y, gt1, sh, sc, g, wout_bf, wr_hi, wr_lo, br]
    specs = [tile, tile, tile, tile, perb, perb, perb] + [_full_spec(p) for p in (g, wout_bf, wr_hi, wr_lo, br)]
    return pl.pallas_call(
        functools.partial(_mixer_post_kernel, bb, tm, d),
        out_shape=[jax.ShapeDtypeStruct((bsz, t, d), F32), jax.ShapeDtypeStruct((bsz, t, d), F32),
                   jax.ShapeDtypeStruct((bsz, t, ROUTER_LANES), F32)],
        grid=(bsz // bb, t // tm),
        in_specs=specs,
        out_specs=[tile, tile, pl.BlockSpec((bb, tm, ROUTER_LANES), lambda b, i: (b, i, 0))],
        compiler_params=_params(("arbitrary", "arbitrary")),
        name="mixer_post",
    )(*ins)


def _route_tables(route, tm):
    n = route.shape[0]
    a_tot = 2 * n
    nt = a_tot // tm + N_EXPERTS
    ef = jnp.concatenate([route[:, ROUTE_E1], route[:, ROUTE_E2]]).astype(jnp.int32)
    oh = (ef[:, None] == jnp.arange(N_EXPERTS, dtype=jnp.int32)[None, :]).astype(jnp.int32)
    csum = jnp.cumsum(oh, axis=0)
    rank = jnp.sum(oh * csum, axis=1) - 1
    cnt = csum[-1]
    ntile = (cnt + tm - 1) // tm
    tile_end = jnp.cumsum(ntile)
    tile_start = tile_end - ntile
    pos = jnp.sum(oh * (tile_start * tm)[None, :], axis=1) + rank
    tiles = jnp.arange(nt, dtype=jnp.int32)
    tile_expert = jnp.minimum(jnp.searchsorted(tile_end, tiles, side="right"), N_EXPERTS - 1).astype(jnp.int32)
    nvalid = jnp.clip(cnt[tile_expert] - (tiles - tile_start[tile_expert]) * tm, 0, tm)
    nvalid = jnp.where(tiles < tile_end[-1], nvalid, 0).astype(jnp.int32)
    a_idx = jnp.arange(a_tot, dtype=jnp.int32)
    src = jnp.zeros((nt * tm,), jnp.int32).at[pos].set(a_idx % n)
    dst = jnp.zeros((nt * tm,), jnp.int32).at[pos].set(a_idx)
    return tile_expert, tile_end[-1:].astype(jnp.int32), nvalid, src, dst


def _moe_routed_kernel(tm, nt, te_ref, nused_ref, nvalid_ref, src_ref, dst_ref, h_hbm, wg_ref, wu_ref, wd_ref,
                       y_hbm, xbuf, ybuf, gsem, ssem):
    i = pl.program_id(0)
    nu = nused_ref[0]
    slot = lax.rem(i, 2)

    def row_copies(tile, sl, gather, start):
        def one(j, priority):
            if gather:
                cp = pltpu.make_async_copy(h_hbm.at[pl.ds(src_ref[tile * tm + j], 1)],
                                           xbuf.at[sl, pl.ds(j, 1)], gsem.at[sl])
            else:
                cp = pltpu.make_async_copy(ybuf.at[sl, pl.ds(j, 1)],
                                           y_hbm.at[pl.ds(dst_ref[tile * tm + j], 1)], ssem.at[sl])
            if start:
                cp.start(priority=priority)
            else:
                cp.wait()

        nv = nvalid_ref[tile]

        def pair(j2, c):
            one(j2 * 2, 0)
            one(j2 * 2 + 1, 1)
            return c
        lax.fori_loop(0, lax.shift_right_logical(nv, 1), pair, 0)

        @pl.when(lax.rem(nv, 2) == 1)
        def _():
            one(nv - 1, 0)

    @pl.when(i == 0)
    def _():
        xbuf[...] = jnp.zeros_like(xbuf)

    @pl.when((i == 0) & (nu > 0))
    def _():
        row_copies(0, 0, True, True)

    @pl.when(i + 1 < nu)
    def _():
        row_copies(i + 1, 1 - slot, True, True)

    @pl.when(i < nu)
    def _():
        row_copies(i, slot, True, False)

        @pl.when(i >= 2)
        def _():
            row_copies(i - 2, slot, False, False)

        x = xbuf[slot].astype(BF16)
        hg = _dot(x, wg_ref[0].astype(BF16))
        hu = _dot(x, wu_ref[0].astype(BF16))
        act = hg * _sigmoid(hg) * hu
        ybuf[slot] = _dot(act.astype(BF16), wd_ref[0].astype(BF16))
        row_copies(i, slot, False, True)

    @pl.when(i == nt - 1)
    def _():
        @pl.when(nu >= 1)
        def _():
            row_copies(nu - 1, lax.rem(nu - 1, 2), False, False)

        @pl.when(nu >= 2)
        def _():
            row_copies(nu - 2, lax.rem(nu, 2), False, False)


def _moe_routed(h2, route, wg, wu, wd):
    n, d = h2.shape
    _, _, f = wg.shape
    tm = MOE_ROW_TILE if n >= 32 * MOE_ROW_TILE else MOE_ROW_TILE_SMALL
    te, nused, nvalid, src, dst = _route_tables(route, tm)
    nt = te.shape[0]
    grid_spec = pltpu.PrefetchScalarGridSpec(
        num_scalar_prefetch=5,
        grid=(nt,),
        in_specs=[
            pl.BlockSpec(memory_space=pl.ANY),
            pl.BlockSpec((1, d, f), lambda i, te, *_: (te[i], 0, 0)),
            pl.BlockSpec((1, d, f), lambda i, te, *_: (te[i], 0, 0)),
            pl.BlockSpec((1, f, d), lambda i, te, *_: (te[i], 0, 0)),
        ],
        out_specs=pl.BlockSpec(memory_space=pl.ANY),
        scratch_shapes=[pltpu.VMEM((2, tm, d), F32), pltpu.VMEM((2, tm, d), F32),
                        pltpu.SemaphoreType.DMA((2,)), pltpu.SemaphoreType.DMA((2,))],
    )
    return pl.pallas_call(
        functools.partial(_moe_routed_kernel, tm, nt),
        out_shape=jax.ShapeDtypeStruct((2 * n, d), F32),
        grid_spec=grid_spec,
        compiler_params=_params(("arbitrary",)),
        name="moe_routed",
    )(te, nused, nvalid, src, dst, h2, wg, wu, wd)


def _final_kernel(bb, tm, d, xa_ref, y0_ref, y1_ref, route_ref, gtp_ref, g_ref, o_ref):
    x = _moe_residual(xa_ref[...].reshape(bb * tm, d), y0_ref, y1_ref, route_ref, _per_seq(gtp_ref, bb, tm, d))
    o_ref[...] = _rmsnorm(x, g_ref[...]).reshape(bb, tm, d)


def _final_norm(xa, res, g):
    bsz, t, d = xa.shape
    bb, tm = _row_tiling(bsz, t)
    tile = pl.BlockSpec((bb, tm, d), lambda b, i: (b, i, 0))
    res_ins, res_specs = _moe_residual_operands(res, bsz, t, bb, tm)
    return pl.pallas_call(
        functools.partial(_final_kernel, bb, tm, d),
        out_shape=jax.ShapeDtypeStruct((bsz, t, d), F32),
        grid=(bsz // bb, t // tm),
        in_specs=[tile] + res_specs + [_full_spec(g)],
        out_specs=tile,
        compiler_params=_params(("arbitrary", "arbitrary")),
        name="final_norm",
    )(xa, *res_ins, g)


def _to_lanes(z, n_heads):
    bsz, t, _ = z.shape
    return z.reshape(bsz, t, n_heads, HEAD_DIM).transpose(1, 3, 0, 2).reshape(t, HEAD_DIM, bsz * n_heads)


def _from_lanes(z, bsz, n_heads):
    t = z.shape[0]
    return z.reshape(t, HEAD_DIM, bsz, n_heads).transpose(2, 0, 3, 1).reshape(bsz, t, n_heads * HEAD_DIM)


def _head_param(p, n_heads):
    return jnp.tile(p.reshape(n_heads, HEAD_DIM).T, (1, V7X_LANES // n_heads))


def _trunk(x, c, shift0, conv0, wkv0, W):
    bsz, t, d = x.shape
    n_heads = d // HEAD_DIM
    depth = W["w_in_bf"].shape[0]
    mod = _modulation(c, W["ada_w"], W["ada_b"])
    bufs, shifts, states = [], [], []
    res = None
    xa = x
    for l in range(depth):
        sh1, sc1, gt1, sh2, sc2, gt2 = (mod[l, :, i * d:(i + 1) * d][:, None, :] for i in range(6))
        row = lambda p: p[l][None, :]
        pre = _mixer_pre(
            xa, res, sh1, sc1, row(W["norm_mix_g"]), W["w_in_bf"][l], shift0[l][:, None, :], conv0[l],
            row(W["mu_shift"]), W["conv_w"][l], row(W["conv_b"]), row(W["conv_ln_g"]), row(W["conv_ln_b"]),
            row(W["rw_w0"]), W["lora_bf"][l], row(W["rw_a0"]), W["g_up_bf"][l])
        if res is not None:
            xa, *pre = pre
        yag, gb, r, lw, k, v, a, convn, shiftn = pre
        head_params = (W["rw_k_k"][l], W["rw_k_a"][l], W["rw_r_k"][l].reshape(-1), W["rw_lnx_g"][l], W["rw_lnx_b"][l])
        if wkv0 is None:
            y, s_l = _wkv_chunked(r, lw, k, v, a, *(p[None, :] for p in head_params))
            states.append(s_l)
        else:
            s0 = wkv0[l].transpose(2, 3, 0, 1).reshape(HEAD_DIM, HEAD_DIM, bsz * n_heads)
            y_l, s_l = _wkv_scan(*(_to_lanes(z, n_heads) for z in (r, lw, k, v, a)),
                                 *(_head_param(p, n_heads) for p in head_params), s0)
            y = _from_lanes(y_l, bsz, n_heads)
            states.append(s_l.reshape(HEAD_DIM, HEAD_DIM, bsz, n_heads).transpose(2, 3, 0, 1))
        x1, h2, route = _mixer_post(xa, yag, gb, y, gt1, sh2, sc2, row(W["norm_ffn_g"]), W["w_out_bf"][l],
                                    W["wr_hi"][l], W["wr_lo"][l], W["br"][l])
        ybuf = _moe_routed(h2.reshape(bsz * t, d), route.reshape(bsz * t, ROUTER_LANES),
                           W["moe_w_gate"][l], W["moe_w_up"][l], W["moe_w_down"][l])
        xa, res = x1, (ybuf, route, gt2)
        bufs.append(convn)
        shifts.append(shiftn[:, 0, :])
    y_out = _final_norm(xa, res, W["final_norm_g"][None, :])
    return y_out, jnp.stack(bufs), jnp.stack(shifts), jnp.stack(states)


def kernel(x_prompt, x_sample, state_conv, state_shift, state_wkv, c_prompt, c_sample, norm_mix_g, norm_ffn_g, final_norm_g, ada_w, ada_b, w_in, mu_shift, conv_w, conv_b, conv_ln_g, conv_ln_b, rw_w0, rw_w_up, rw_a0, rw_a_up, rw_g_up, rw_k_k, rw_k_a, rw_r_k, rw_lnx_g, rw_lnx_b, w_out, moe_w_group, moe_b_group, moe_w_expert, moe_b_expert, moe_w_gate, moe_w_up, moe_w_down):
    depth, d, _ = w_in.shape
    zw = jnp.zeros_like(rw_w_up)
    lora = jnp.concatenate([jnp.concatenate([rw_w_up, zw], axis=2), jnp.concatenate([zw, rw_a_up], axis=2)], axis=1)
    pad = ROUTER_LANES - N_EXPERTS - N_GROUPS
    wr = jnp.concatenate([moe_w_expert, moe_w_group, jnp.zeros((depth, d, pad), F32)], axis=2)
    wr_hi = wr.astype(BF16)
    br = jnp.concatenate([moe_b_expert, moe_b_group, jnp.zeros((depth, pad), F32)], axis=1)[:, None, :]
    W = dict(norm_mix_g=norm_mix_g, norm_ffn_g=norm_ffn_g, final_norm_g=final_norm_g, ada_w=ada_w, ada_b=ada_b,
             w_in_bf=w_in.astype(BF16), mu_shift=mu_shift, conv_w=conv_w, conv_b=conv_b, conv_ln_g=conv_ln_g,
             conv_ln_b=conv_ln_b, rw_w0=rw_w0, lora_bf=lora.astype(BF16), rw_a0=rw_a0, g_up_bf=rw_g_up.astype(BF16),
             rw_k_k=rw_k_k, rw_k_a=rw_k_a, rw_r_k=rw_r_k, rw_lnx_g=rw_lnx_g, rw_lnx_b=rw_lnx_b,
             w_out_bf=w_out.astype(BF16), wr_hi=wr_hi, wr_lo=(wr - wr_hi.astype(F32)).astype(BF16), br=br,
             moe_w_gate=moe_w_gate, moe_w_up=moe_w_up, moe_w_down=moe_w_down)
    bp = x_prompt.shape[0]
    conv0 = jnp.zeros((depth, bp, CONV_BUF, d), x_prompt.dtype)
    shift0 = jnp.zeros((depth, bp, d), x_prompt.dtype)
    y_p, conv_p, shift_p, wkv_p = _trunk(x_prompt, c_prompt, shift0, conv0, None, W)
    y_s, conv_s, shift_s, wkv_s = _trunk(x_sample, c_sample, state_shift, state_conv, state_wkv, W)
    return (y_p, y_s, conv_p, shift_p, wkv_p.astype(state_wkv.dtype), conv_s, shift_s, wkv_s.astype(state_wkv.dtype))
```

```python
import functools

import jax
import jax.numpy as jnp
from jax import lax
from jax.experimental import pallas as pl
from jax.experimental.pallas import tpu as pltpu

F32 = jnp.float32
BF16 = jnp.bfloat16

HEAD_DIM = 64
CONV_WIDTH = 31
CONV_BUF = CONV_WIDTH - 1
LORA_W = 64
LORA_A = 64
N_GROUPS = 4
EXPERTS_PER_GROUP = 8
N_EXPERTS = N_GROUPS * EXPERTS_PER_GROUP
RMS_EPS = 1e-6
LN_EPS = 1e-5
GN_EPS = 64e-5

V7X_LANES = 128
V7X_SUBLANES = 8
V7X_VMEM_LIMIT_BYTES = 56 * 1024 * 1024

ROW_TILE = 256
SHORT_SEQ_ROWS = 128
HIST_ROWS = 32
HIST_PAD = HIST_ROWS - CONV_BUF
CONV_LANE_CHUNK = 256
WKV_CHUNK = 64
WKV_GROUP_HEADS = 4
WKV_GROUP = WKV_GROUP_HEADS * HEAD_DIM
ROUTER_LANES = V7X_LANES
ROUTE_E1, ROUTE_E2, ROUTE_W1, ROUTE_W2, ROUTE_R1, ROUTE_R2 = 0, 1, 2, 3, 4, 5
MOE_ROW_TILE = 256
MOE_ROW_TILE_SMALL = 64
MOE_SCATTER_TOKENS = 256
MOE_COMBINE_TOKENS = 128


def _dot(a, b):
    return jnp.dot(a, b, preferred_element_type=F32)


def _sigmoid(x):
    return 1.0 / (1.0 + jnp.exp(-x))


def _rmsnorm(x, g):
    return x * lax.rsqrt(jnp.mean(x * x, axis=-1, keepdims=True) + RMS_EPS) * g


def _params(sem):
    return pltpu.CompilerParams(dimension_semantics=sem, vmem_limit_bytes=V7X_VMEM_LIMIT_BYTES)


def _row_tiling(bsz, t):
    tm = min(t, ROW_TILE)
    bb = max(1, min(bsz, SHORT_SEQ_ROWS // tm))
    assert t % tm == 0 and bsz % bb == 0 and tm % V7X_SUBLANES == 0
    assert bb == 1 or tm == t
    return bb, tm


def _per_seq(ref, bb, n, d):
    p = ref[...]
    if bb == 1:
        return p[0]
    return jnp.broadcast_to(p, (bb, n, d)).reshape(bb * n, d)


def _full_spec(a):
    return pl.BlockSpec(a.shape, lambda b, i: (0,) * a.ndim)


def _moe_residual_operands(res, bb, tm):
    moe, gate = res
    d = moe.shape[-1]
    return ([moe, gate], [pl.BlockSpec((bb, tm, d), lambda b, i: (b, i, 0)),
                          pl.BlockSpec((bb, 1, d), lambda b, i: (b, 0, 0))])


def _ada_kernel(c_ref, w_ref, b_ref, o_ref):
    c = c_ref[...]
    s = c * _sigmoid(c)
    o_ref[0] = _dot(s.astype(BF16), w_ref[0].astype(BF16)) + b_ref[0]


def _modulation(c, ada_w, ada_b):
    depth, d, n = ada_w.shape
    bsz = c.shape[0]
    tn = 1536
    return pl.pallas_call(
        _ada_kernel,
        out_shape=jax.ShapeDtypeStruct((depth, bsz, n), F32),
        grid=(depth, n // tn),
        in_specs=[
            pl.BlockSpec((bsz, d), lambda l, j: (0, 0)),
            pl.BlockSpec((1, d, tn), lambda l, j: (l, 0, j)),
            pl.BlockSpec((1, 1, tn), lambda l, j: (l, 0, j)),
        ],
        out_specs=pl.BlockSpec((1, bsz, tn), lambda l, j: (l, 0, j)),
        compiler_params=_params(("arbitrary", "arbitrary")),
        name="ada_modulation",
    )(c, ada_w, ada_b.reshape(depth, 1, n))


def _mixer_pre_kernel(has_res, bb, tm, d, *refs):
    if has_res:
        xa_ref, moe_ref, gtp_ref, *refs = refs
    else:
        xa_ref, *refs = refs
    (sh_ref, sc_ref, g_ref, win_ref, shift0_ref, conv0_ref, mu_ref, cw_ref, cb_ref, clg_ref, clb_ref,
     w0_ref, lora_ref, a0_ref, gup_ref, *refs) = refs
    if has_res:
        xres_ref, *refs = refs
    (yag_ref, gb_ref, r_ref, lw_ref, k_ref, v_ref, a_ref, convn_ref, shiftn_ref,
     ubuf_ref, z_ref, carry_ref, shifted_ref) = refs
    c_plain = 4 * d
    rows = bb * tm
    t = pl.program_id(1)
    nt = pl.num_programs(1)

    x = xa_ref[...].reshape(rows, d)
    if has_res:
        x = x + _per_seq(gtp_ref, bb, tm, d) * moe_ref[...].reshape(rows, d)
        xres_ref[...] = x.reshape(bb, tm, d)
    h = _rmsnorm(x, g_ref[...]) * (1.0 + _per_seq(sc_ref, bb, tm, d)) + _per_seq(sh_ref, bb, tm, d)
    proj = _dot(h.astype(BF16), win_ref[...])

    @pl.when(t == 0)
    def _():
        hp = _per_seq(shift0_ref, bb, V7X_SUBLANES, d)
        hp = jnp.broadcast_to(hp, (bb * V7X_SUBLANES, d)).astype(BF16)
        carry_ref[...] = _dot(hp, win_ref[:, c_plain:])
        ubuf_ref[:, pl.ds(0, HIST_PAD), :] = jnp.zeros((bb, HIST_PAD, d), F32)
        ubuf_ref[:, pl.ds(HIST_PAD, CONV_BUF), :] = conv0_ref[...]

    @pl.when(t == nt - 1)
    def _():
        shiftn_ref[...] = h.reshape(bb, tm, d)[:, tm - 1:tm, :]

    cur = proj[:, c_plain:]
    prev = pltpu.roll(cur, 1, 0)
    row = lax.broadcasted_iota(jnp.int32, cur.shape, 0)
    if bb == 1:
        prev = jnp.where(row == 0, carry_ref[pl.ds(0, 1), :], prev)
        carry_ref[pl.ds(0, 1), :] = cur[tm - 1:tm, :]
    else:
        prev = jnp.where((row & (tm - 1)) == 0, carry_ref[...], prev)
    rw = cur + (prev - cur) * mu_ref[...]

    r = rw[:, 0:d]
    k = rw[:, d:2 * d]
    v = rw[:, 2 * d:3 * d]
    xwa = rw[:, 3 * d:3 * d + LORA_W + LORA_A]
    xg = rw[:, 3 * d + LORA_W + LORA_A:]
    lane = lax.broadcasted_iota(jnp.int32, xwa.shape, 1)
    lhs = jnp.where(lane < LORA_W, jnp.tanh(xwa), xwa)
    za = _dot(lhs.astype(BF16), lora_ref[...])
    log_decay = -jnp.exp(-0.5) * _sigmoid(w0_ref[...] + za[:, 0:d])
    a = _sigmoid(a0_ref[...] + za[:, d:])
    gate = _dot(_sigmoid(xg).astype(BF16), gup_ref[...])
    blk = (bb, tm, d)
    r_ref[...] = r.reshape(blk)
    k_ref[...] = k.reshape(blk)
    v_ref[...] = v.reshape(blk)
    lw_ref[...] = log_decay.reshape(blk)
    a_ref[...] = a.reshape(blk)
    gb_ref[...] = (_sigmoid(proj[:, 3 * d:4 * d]) * gate).reshape(blk)

    u = proj[:, 0:d] * _sigmoid(proj[:, d:2 * d])
    ubuf_ref[:, pl.ds(HIST_ROWS, tm), :] = u.reshape(blk)
    rb = min(tm, 32)
    lc = CONV_LANE_CHUNK
    span = tm + HIST_ROWS - V7X_SUBLANES
    for b in range(bb):
        for c0 in range(0, d, lc):
            for s in range(1, V7X_SUBLANES):
                shifted_ref[s - 1] = ubuf_ref[b, pl.ds(s, span), c0:c0 + lc]
            for r0 in range(0, tm, rb):
                acc = jnp.broadcast_to(cb_ref[:, c0:c0 + lc], (rb, lc))
                for tap in range(CONV_WIDTH):
                    s = (HIST_PAD + tap) % V7X_SUBLANES
                    base = r0 + HIST_PAD + tap - s
                    if s == 0:
                        src = ubuf_ref[b, pl.ds(base, rb), c0:c0 + lc]
                    else:
                        src = shifted_ref[s - 1, pl.ds(base, rb), :]
                    acc = acc + cw_ref[pl.ds(tap, 1), c0:c0 + lc] * src
                z_ref[pl.ds(b * tm + r0, rb), c0:c0 + lc] = acc
    z = z_ref[...]
    mu = jnp.mean(z, axis=-1, keepdims=True)
    zc = z - mu
    var = jnp.mean(zc * zc, axis=-1, keepdims=True)
    zn = zc * lax.rsqrt(var + LN_EPS) * clg_ref[...] + clb_ref[...]
    yag_ref[...] = (_sigmoid(proj[:, 2 * d:3 * d]) * (zn * _sigmoid(zn))).reshape(blk)

    @pl.when(t == nt - 1)
    def _():
        convn_ref[...] = ubuf_ref[:, pl.ds(tm + HIST_PAD, CONV_BUF), :]

    if tm >= HIST_ROWS:
        @pl.when(t < nt - 1)
        def _():
            ubuf_ref[:, pl.ds(0, HIST_ROWS), :] = ubuf_ref[:, pl.ds(tm, HIST_ROWS), :]


def _mixer_pre(xa, res, sh, sc, g, win_bf, shift0, conv0, mu, cw, cb, clg, clb, w0, lora_bf, a0, gup_bf):
    bsz, t, d = xa.shape
    bb, tm = _row_tiling(bsz, t)
    assert t == tm or tm >= HIST_ROWS
    assert bb == 1 or tm == V7X_SUBLANES
    c_rwkv = win_bf.shape[1] - 4 * d
    tile = pl.BlockSpec((bb, tm, d), lambda b, i: (b, i, 0))
    perb = pl.BlockSpec((bb, 1, d), lambda b, i: (b, 0, 0))
    hist = pl.BlockSpec((bb, CONV_BUF, d), lambda b, i: (b, 0, 0))
    ins, specs = [xa], [tile]
    n_act = 7
    if res is not None:
        res_ins, res_specs = _moe_residual_operands(res, bb, tm)
        ins += res_ins
        specs += res_specs
        n_act += 1
    ins += [sh, sc, g, win_bf, shift0, conv0, mu, cw, cb, clg, clb, w0, lora_bf, a0, gup_bf]
    specs += [perb, perb, _full_spec(g),
              pl.BlockSpec(win_bf.shape, lambda b, i: (0, 0), pipeline_mode=pl.Buffered(1)),
              perb, hist] + [_full_spec(p) for p in (mu, cw, cb, clg, clb, w0, lora_bf, a0, gup_bf)]
    act = jax.ShapeDtypeStruct((bsz, t, d), F32)
    return pl.pallas_call(
        functools.partial(_mixer_pre_kernel, res is not None, bb, tm, d),
        out_shape=[act] * n_act + [jax.ShapeDtypeStruct((bsz, CONV_BUF, d), F32),
                                   jax.ShapeDtypeStruct((bsz, 1, d), F32)],
        grid=(bsz // bb, t // tm),
        in_specs=specs,
        out_specs=[tile] * n_act + [hist, perb],
        scratch_shapes=[pltpu.VMEM((bb, tm + HIST_ROWS, d), F32), pltpu.VMEM((bb * tm, d), F32),
                        pltpu.VMEM((bb * V7X_SUBLANES, c_rwkv), F32),
                        pltpu.VMEM((V7X_SUBLANES - 1, tm + HIST_ROWS - V7X_SUBLANES, CONV_LANE_CHUNK), F32)],
        compiler_params=_params(("arbitrary", "arbitrary")),
        name="mixer_pre",
    )(*ins)


def _wkv_kernel(tc, r_ref, lw_ref, k_ref, v_ref, a_ref, kkp_ref, kap_ref, rkp_ref, lng_ref, lnb_ref, s0_ref,
                y_ref, s_ref, o_ref):
    n = HEAD_DIM

    @pl.when(pl.program_id(1) == 0)
    def _():
        s_ref[...] = s0_ref[...]

    def step(tt, carry):
        r_t = r_ref[tt]
        k_t = k_ref[tt]
        a_t = a_ref[tt]
        w_t = jnp.exp(lw_ref[tt])
        kk = k_t * kkp_ref[...]
        kk = kk * lax.rsqrt(jnp.sum(kk * kk, axis=0, keepdims=True) + 1e-12)
        b_t = kk * a_t
        k2 = k_t * (1.0 + (a_t - 1.0) * kap_ref[...])

        def row(i, c):
            si = s_ref[i]
            sa = jnp.sum(si * kk, axis=0, keepdims=True)
            vi = v_ref[tt, pl.ds(i, 1), :]
            sn = si * w_t - sa * b_t + vi * k2
            s_ref[i] = sn
            o_ref[pl.ds(i, 1), :] = jnp.sum(sn * r_t, axis=0, keepdims=True)
            return c

        lax.fori_loop(0, n, row, 0, unroll=8)
        o = o_ref[...]
        mu = jnp.mean(o, axis=0, keepdims=True)
        oc = o - mu
        var = jnp.mean(oc * oc, axis=0, keepdims=True)
        yn = oc * lax.rsqrt(var + GN_EPS) * lng_ref[...] + lnb_ref[...]
        bonus = jnp.sum(r_t * k2 * rkp_ref[...], axis=0, keepdims=True) * v_ref[tt]
        y_ref[tt] = yn + bonus
        return carry

    lax.fori_loop(0, tc, step, 0)


def _wkv_scan(r, w, k, v, a, kkp, kap, rkp, lng, lnb, s0):
    t, n, nl = r.shape
    tc = min(t, 32)
    lb = V7X_LANES
    seq = pl.BlockSpec((tc, n, lb), lambda l, i: (i, 0, l))
    par = pl.BlockSpec((n, lb), lambda l, i: (0, 0))
    st = pl.BlockSpec((n, n, lb), lambda l, i: (0, 0, l))
    return pl.pallas_call(
        functools.partial(_wkv_kernel, tc),
        out_shape=[jax.ShapeDtypeStruct((t, n, nl), F32), jax.ShapeDtypeStruct((n, n, nl), F32)],
        grid=(nl // lb, t // tc),
        in_specs=[seq] * 5 + [par] * 5 + [st],
        out_specs=[seq, st],
        scratch_shapes=[pltpu.VMEM((n, lb), F32)],
        compiler_params=_params(("arbitrary", "arbitrary")),
        name="wkv_scan",
    )(r, w, k, v, a, kkp, kap, rkp, lng, lnb, s0)


def _bmm(a, b):
    return jnp.einsum("gck,gkn->gcn", a, b, preferred_element_type=F32)


def _bmm_nt(a, b):
    return jnp.einsum("gck,gnk->gcn", a, b, preferred_element_type=F32)


def _bmm_tn(a, b):
    return jnp.einsum("gtc,gtn->gcn", a, b, preferred_element_type=F32)


def _split_bf16(x):
    hi = x.astype(BF16)
    return hi, (x - hi.astype(F32)).astype(BF16)


def _wkv_chunk_kernel(tt, d, r_ref, lw_ref, k_ref, v_ref, a_ref, kkp_ref, kap_ref, rkp_ref, lng_ref, lnb_ref,
                      ones_ref, tri_ref, y_ref, sout_ref, s_ref):
    c = WKV_CHUNK
    gw = WKV_GROUP
    ng = d // gw
    nc = tt // c
    n_items = nc * ng
    t = pl.program_id(1)
    nt = pl.num_programs(1)

    @pl.when(t == 0)
    def _():
        s_ref[...] = jnp.zeros_like(s_ref)

    lane = lax.broadcasted_iota(jnp.int32, (1, c, gw), 2)
    rowi = lax.broadcasted_iota(jnp.int32, (1, c, gw), 1)
    head_of_lane = lax.shift_right_logical(lane, 6)
    hmask = [head_of_lane == h for h in range(WKV_GROUP_HEADS)]
    tcol = lane & (c - 1)
    strict = tcol < rowi
    incl = tcol <= rowi
    bl = lax.broadcasted_iota(jnp.int32, (1, gw, gw), 2)
    br = lax.broadcasted_iota(jnp.int32, (1, gw, gw), 1)
    blockmask = lax.shift_right_logical(bl, 6) == lax.shift_right_logical(br, 6)

    def items(ref):
        x = ref[0].reshape(nc, c, d)
        return jnp.stack([x[:, :, g * gw:(g + 1) * gw] for g in range(ng)], axis=1).reshape(n_items, c, gw)

    def per_group(ref):
        p = jnp.stack([ref[:, g * gw:(g + 1) * gw] for g in range(ng)], axis=0)
        return jnp.broadcast_to(p[None], (nc, ng, 1, gw)).reshape(n_items, 1, gw)

    def bd(xf):
        return jnp.concatenate([jnp.where(hmask[h], xf, 0.0) for h in range(WKV_GROUP_HEADS)], axis=1).astype(BF16)

    def segsum(x):
        shp = x.shape
        hi, lo = _split_bf16(x.reshape(-1, gw))
        return (_dot(hi, ones_ref[...]) + _dot(lo, ones_ref[...])).reshape(shp)

    r, lw, k, v, a = (items(ref) for ref in (r_ref, lw_ref, k_ref, v_ref, a_ref))
    kk = k * per_group(kkp_ref)
    kk = kk * lax.rsqrt(segsum(kk * kk) + 1e-12)
    b = kk * a
    k2 = k * (1.0 + (a - 1.0) * per_group(kap_ref))
    lw_hi, lw_lo = _split_bf16(lw)
    tri = jnp.broadcast_to(tri_ref[...][None], (n_items, c, c))
    cum = _bmm(tri, lw_hi) + _bmm(tri, lw_lo)
    cl = cum[:, c - 1:c, :]
    at = -(kk * jnp.exp(cum - lw))
    rt = r * jnp.exp(cum)
    winv = jnp.exp(-cum)
    wrem = jnp.exp(cl - cum)
    wc = jnp.exp(cl)
    lhs = jnp.concatenate([at, rt], axis=1).astype(BF16)
    rhs = jnp.concatenate([bd(b * winv), bd(k2 * winv)], axis=1)
    sc = _bmm_nt(lhs, rhs)
    l_ab = jnp.where(strict, sc[:, 0:c, 0:gw], 0.0)
    l_ak = jnp.where(strict, sc[:, 0:c, gw:], 0.0)
    l_rb = jnp.where(incl, sc[:, c:, 0:gw], 0.0)
    l_rk = jnp.where(incl, sc[:, c:, gw:], 0.0)
    pk = l_ab
    tm = l_ab
    pbd = bd(pk)
    for _ in range(c.bit_length() - 2):
        pk = _bmm(pk.astype(BF16), pbd)
        pbd = bd(pk)
        tm = tm + pk + _bmm(tm.astype(BF16), pbd)
    tmb = tm.astype(BF16)
    vbd = bd(v)
    ah = at + _bmm(tmb, bd(at))
    x1 = _bmm(l_ak.astype(BF16), vbd)
    gm = x1 + _bmm(tmb, bd(x1))
    l_rbb = l_rb.astype(BF16)
    rh = (rt + _bmm(l_rbb, bd(ah))).astype(BF16)
    o_ind = _bmm(jnp.concatenate([l_rbb, l_rk.astype(BF16)], axis=2), jnp.concatenate([bd(gm), vbd], axis=1))
    bb = (b * wrem).astype(BF16)
    kb = (k2 * wrem).astype(BF16)
    m = jnp.where(blockmask, _bmm_tn(ah.astype(BF16), bb), 0.0).astype(BF16)
    n = jnp.where(blockmask, _bmm_tn(jnp.concatenate([gm, v], axis=1).astype(BF16),
                                     jnp.concatenate([bb, kb], axis=1)), 0.0)

    s = s_ref[...]
    outs = []
    for ci in range(nc):
        sl = slice(ci * ng, (ci + 1) * ng)
        sb = s.astype(BF16)
        outs.append(_bmm_nt(rh[sl], sb) + o_ind[sl])
        s = s * wc[sl] + _bmm(sb, m[sl]) + n[sl]
    s_ref[...] = s

    inv_n = 1.0 / HEAD_DIM
    o = jnp.stack(outs, axis=0).reshape(n_items, c, gw)
    mu = segsum(o) * inv_n
    oc = o - mu
    var = segsum(oc * oc) * inv_n
    yn = oc * lax.rsqrt(var + GN_EPS) * per_group(lng_ref) + per_group(lnb_ref)
    bonus = segsum(r * k2 * per_group(rkp_ref)) * v
    y = (yn + bonus).reshape(nc, ng, c, gw)
    for g in range(ng):
        y_ref[0, :, g * gw:(g + 1) * gw] = y[:, g].reshape(tt, gw)

    @pl.when(t == nt - 1)
    def _():
        for g in range(ng):
            for h in range(WKV_GROUP_HEADS):
                hs = slice(h * HEAD_DIM, (h + 1) * HEAD_DIM)
                sout_ref[0, g * WKV_GROUP_HEADS + h] = s_ref[g, hs, hs]


def _wkv_chunked(r, lw, k, v, a, kkp, kap, rkp, lng, lnb):
    bsz, t, d = r.shape
    tt = min(t, ROW_TILE)
    assert t % tt == 0 and tt % WKV_CHUNK == 0 and d % WKV_GROUP == 0
    n_heads = d // HEAD_DIM
    gw = WKV_GROUP
    li = lax.broadcasted_iota(jnp.int32, (gw, gw), 0) // HEAD_DIM
    lj = lax.broadcasted_iota(jnp.int32, (gw, gw), 1) // HEAD_DIM
    ones_bd = (li == lj).astype(BF16)
    ti = lax.broadcasted_iota(jnp.int32, (WKV_CHUNK, WKV_CHUNK), 0)
    tj = lax.broadcasted_iota(jnp.int32, (WKV_CHUNK, WKV_CHUNK), 1)
    tri = (ti >= tj).astype(BF16)
    tile = pl.BlockSpec((1, tt, d), lambda b, i: (b, i, 0))
    par = pl.BlockSpec((1, d), lambda b, i: (0, 0))
    return pl.pallas_call(
        functools.partial(_wkv_chunk_kernel, tt, d),
        out_shape=[jax.ShapeDtypeStruct((bsz, t, d), F32),
                   jax.ShapeDtypeStruct((bsz, n_heads, HEAD_DIM, HEAD_DIM), F32)],
        grid=(bsz, t // tt),
        in_specs=[tile] * 5 + [par] * 5 + [_full_spec(ones_bd), _full_spec(tri)],
        out_specs=[tile, pl.BlockSpec((1, n_heads, HEAD_DIM, HEAD_DIM), lambda b, i: (b, 0, 0, 0))],
        scratch_shapes=[pltpu.VMEM((d // gw, gw, gw), F32)],
        compiler_params=_params(("arbitrary", "arbitrary")),
        name="wkv_chunked",
    )(r, lw, k, v, a, kkp, kap, rkp, lng, lnb, ones_bd, tri)


def _mixer_post_kernel(bb, tm, d, x_ref, yag_ref, gb_ref, y_ref, gt1_ref, sh_ref, sc_ref, g_ref, wout_ref,
                       wrh_ref, wrl_ref, br_ref, tri_ref, x1_ref, h2_ref, route_ref, counts_ref, cnt_ref):
    rows = bb * tm

    @pl.when((pl.program_id(0) == 0) & (pl.program_id(1) == 0))
    def _():
        cnt_ref[...] = jnp.zeros_like(cnt_ref)

    x = x_ref[...].reshape(rows, d)
    merged = (yag_ref[...] + gb_ref[...] * y_ref[...]).reshape(rows, d)
    x1 = x + _per_seq(gt1_ref, bb, tm, d) * _dot(merged.astype(BF16), wout_ref[...])
    x1_ref[...] = x1.reshape(bb, tm, d)
    h2 = _rmsnorm(x1, g_ref[...]) * (1.0 + _per_seq(sc_ref, bb, tm, d)) + _per_seq(sh_ref, bb, tm, d)
    h2_ref[...] = h2.reshape(bb, tm, V7X_SUBLANES, d // V7X_SUBLANES)
    h_hi = h2.astype(BF16)
    h_lo = (h2 - h_hi.astype(F32)).astype(BF16)
    logits = _dot(h_hi, wrh_ref[...]) + (_dot(h_lo, wrh_ref[...]) + _dot(h_hi, wrl_ref[...])) + br_ref[...]

    lane_i = lax.broadcasted_iota(jnp.int32, logits.shape, 1)
    lane = lane_i.astype(F32)
    grp_of_lane = lax.shift_right_logical(lane_i, 3).astype(F32)
    neg = -jnp.inf
    none = 1e9
    is_g = (lane_i >= N_EXPERTS) & (lane_i < N_EXPERTS + N_GROUPS)
    glv = jnp.where(is_g, logits, neg)
    gmax = jnp.max(glv, axis=-1, keepdims=True)
    gidx = jnp.min(jnp.where(is_g & (glv == gmax), lane - N_EXPERTS, none), axis=-1, keepdims=True)
    pg = 1.0 / jnp.sum(jnp.where(is_g, jnp.exp(glv - gmax), 0.0), axis=-1, keepdims=True)
    in_grp = (lane_i < N_EXPERTS) & (grp_of_lane == gidx)
    ev = jnp.where(in_grp, logits, neg)
    v1 = jnp.max(ev, axis=-1, keepdims=True)
    i1 = jnp.min(jnp.where(in_grp & (ev == v1), lane, none), axis=-1, keepdims=True)
    ev2 = jnp.where(lane == i1, neg, ev)
    v2 = jnp.max(ev2, axis=-1, keepdims=True)
    i2 = jnp.min(jnp.where(in_grp & (ev2 == v2) & (lane != i1), lane, none), axis=-1, keepdims=True)
    e2 = jnp.exp(v2 - v1)
    den = 1.0 + e2
    hit = ((lane == i1) | (lane == i2)).astype(F32)
    before = cnt_ref[pl.ds(0, 1), :] + _dot(tri_ref[...], hit.astype(BF16))
    r1 = jnp.sum(jnp.where(lane == i1, before, 0.0), axis=-1, keepdims=True)
    r2 = jnp.sum(jnp.where(lane == i2, before, 0.0), axis=-1, keepdims=True)
    counts = cnt_ref[pl.ds(0, 1), :] + jnp.sum(hit, axis=0, keepdims=True)
    cnt_ref[pl.ds(0, 1), :] = counts
    counts_ref[...] = jnp.broadcast_to(counts, counts_ref.shape)
    route = (jnp.where(lane_i == ROUTE_E1, i1, 0.0) + jnp.where(lane_i == ROUTE_E2, i2, 0.0)
             + jnp.where(lane_i == ROUTE_W1, (1.0 / den) * pg, 0.0) + jnp.where(lane_i == ROUTE_W2, (e2 / den) * pg, 0.0)
             + jnp.where(lane_i == ROUTE_R1, r1, 0.0) + jnp.where(lane_i == ROUTE_R2, r2, 0.0))
    route_ref[...] = route.reshape(bb, tm, ROUTER_LANES)


def _mixer_post(x, yag, gb, y, gt1, sh, sc, g, wout_bf, wr_hi, wr_lo, br):
    bsz, t, d = x.shape
    bb, tm = _row_tiling(bsz, t)
    rows = bb * tm
    tile = pl.BlockSpec((bb, tm, d), lambda b, i: (b, i, 0))
    perb = pl.BlockSpec((bb, 1, d), lambda b, i: (b, 0, 0))
    ri = lax.broadcasted_iota(jnp.int32, (rows, rows), 0)
    ci = lax.broadcasted_iota(jnp.int32, (rows, rows), 1)
    tri = (ci < ri).astype(BF16)
    ins = [x, yag, gb, y, gt1, sh, sc, g, wout_bf, wr_hi, wr_lo, br, tri]
    specs = [tile, tile, tile, tile, perb, perb, perb] + [_full_spec(p) for p in (g, wout_bf, wr_hi, wr_lo, br, tri)]
    token_tiles = (bb, tm, V7X_SUBLANES, d // V7X_SUBLANES)
    return pl.pallas_call(
        functools.partial(_mixer_post_kernel, bb, tm, d),
        out_shape=[jax.ShapeDtypeStruct((bsz, t, d), F32),
                   jax.ShapeDtypeStruct((bsz, t, V7X_SUBLANES, d // V7X_SUBLANES), F32),
                   jax.ShapeDtypeStruct((bsz, t, ROUTER_LANES), F32),
                   jax.ShapeDtypeStruct((V7X_SUBLANES, ROUTER_LANES), F32)],
        grid=(bsz // bb, t // tm),
        in_specs=specs,
        out_specs=[tile, pl.BlockSpec(token_tiles, lambda b, i: (b, i, 0, 0)),
                   pl.BlockSpec((bb, tm, ROUTER_LANES), lambda b, i: (b, i, 0)),
                   pl.BlockSpec((V7X_SUBLANES, ROUTER_LANES), lambda b, i: (0, 0))],
        scratch_shapes=[pltpu.VMEM((V7X_SUBLANES, ROUTER_LANES), F32)],
        compiler_params=_params(("arbitrary", "arbitrary")),
        name="mixer_post",
    )(*ins)


def _moe_tables(route, counts, tm):
    n = route.shape[0]
    nt = 2 * n // tm
    cnt = counts[:N_EXPERTS].astype(jnp.int32)
    ends = jnp.cumsum(cnt)
    starts = ends - cnt
    ids = jnp.arange(N_EXPERTS, dtype=jnp.int32)[None, :]

    def position(e_lane, r_lane):
        e = route[:, e_lane].astype(jnp.int32)
        return jnp.sum(jnp.where(e[:, None] == ids, starts[None, :], 0), axis=1) + route[:, r_lane].astype(jnp.int32)

    pos = jnp.concatenate([position(ROUTE_E1, ROUTE_R1), position(ROUTE_E2, ROUTE_R2)])
    bounds = jnp.sort(jnp.concatenate([jnp.arange(nt + 1, dtype=jnp.int32) * tm, ends]))
    lo_abs, hi_abs = bounds[:-1], bounds[1:]
    tile = jnp.minimum(lo_abs // tm, nt - 1)
    nonzero = hi_abs > lo_abs
    expert = jnp.minimum(jnp.sum((ends[None, :] <= lo_abs[:, None]).astype(jnp.int32), axis=1), N_EXPERTS - 1)
    expert = jnp.maximum(lax.cummax(jnp.where(nonzero, expert, -1)), 0)
    prev_tile = jnp.concatenate([jnp.full((1,), -1, jnp.int32), lax.cummax(jnp.where(nonzero, tile, -1))[:-1]])
    first = (nonzero & (tile > prev_tile)).astype(jnp.int32)
    return pos, tile, expert, lo_abs - tile * tm, hi_abs - tile * tm, first


def _moe_scatter_kernel(ts, n, pos_ref, h_ref, xs_hbm, sem):
    i = pl.program_id(0)

    def copies(start):
        def body(j, c):
            for s in range(2):
                cp = pltpu.make_async_copy(h_ref.at[j], xs_hbm.at[pos_ref[s * n + i * ts + j]], sem)
                if start:
                    cp.start(priority=s)
                else:
                    cp.wait()
            return c
        lax.fori_loop(0, ts, body, 0, unroll=8)

    copies(True)
    copies(False)


def _moe_scatter(h2t, pos):
    n = h2t.shape[0]
    ts = min(n, MOE_SCATTER_TOKENS)
    grid_spec = pltpu.PrefetchScalarGridSpec(
        num_scalar_prefetch=1, grid=(n // ts,),
        in_specs=[pl.BlockSpec((ts, V7X_SUBLANES, V7X_LANES), lambda i, p: (i, 0, 0))],
        out_specs=pl.BlockSpec(memory_space=pl.ANY),
        scratch_shapes=[pltpu.SemaphoreType.DMA(())])
    return pl.pallas_call(
        functools.partial(_moe_scatter_kernel, ts, n),
        out_shape=jax.ShapeDtypeStruct((2 * n, V7X_SUBLANES, V7X_LANES), F32),
        grid_spec=grid_spec, compiler_params=_params(("arbitrary",)), name="moe_scatter")(pos, h2t)


def _moe_experts_kernel(tm, d, tile_ref, exp_ref, lo_ref, hi_ref, first_ref, xs_ref, wg_ref, wu_ref, wd_ref, ys_ref):
    w = pl.program_id(0)
    lo = lo_ref[w]
    hi = hi_ref[w]

    @pl.when(hi > lo)
    def _():
        x = xs_ref[...].reshape(tm, d).astype(BF16)
        hg = _dot(x, wg_ref[0].astype(BF16))
        hu = _dot(x, wu_ref[0].astype(BF16))
        act = hg * _sigmoid(hg) * hu
        y = _dot(act.astype(BF16), wd_ref[0].astype(BF16))
        row = lax.broadcasted_iota(jnp.int32, (tm, 1), 0)
        y = jnp.where((row >= lo) & (row < hi), y, 0.0).reshape(tm, V7X_SUBLANES, V7X_LANES)

        @pl.when(first_ref[w] == 1)
        def _():
            ys_ref[...] = y

        @pl.when(first_ref[w] == 0)
        def _():
            ys_ref[...] += y


def _moe_experts(xs, tile, expert, lo, hi, first, wg, wu, wd, tm):
    d = V7X_SUBLANES * V7X_LANES
    _, _, f = wg.shape
    row_tile = pl.BlockSpec((tm, V7X_SUBLANES, V7X_LANES), lambda w, t, e, *_: (t[w], 0, 0))
    grid_spec = pltpu.PrefetchScalarGridSpec(
        num_scalar_prefetch=5, grid=(tile.shape[0],),
        in_specs=[row_tile,
                  pl.BlockSpec((1, d, f), lambda w, t, e, *_: (e[w], 0, 0)),
                  pl.BlockSpec((1, d, f), lambda w, t, e, *_: (e[w], 0, 0)),
                  pl.BlockSpec((1, f, d), lambda w, t, e, *_: (e[w], 0, 0))],
        out_specs=row_tile)
    return pl.pallas_call(
        functools.partial(_moe_experts_kernel, tm, d),
        out_shape=jax.ShapeDtypeStruct(xs.shape, F32),
        grid_spec=grid_spec, compiler_params=_params(("arbitrary",)), name="moe_experts")(
            tile, expert, lo, hi, first, xs, wg, wu, wd)


def _moe_combine_kernel(tc, n, d, pos_ref, ys_hbm, route_ref, o_ref, gbuf, sem):
    i = pl.program_id(0)
    nt = pl.num_programs(0)
    slot = lax.rem(i, 2)

    def copies(blk, sl, start):
        def body(j, c):
            for s in range(2):
                cp = pltpu.make_async_copy(ys_hbm.at[pos_ref[s * n + blk * tc + j]], gbuf.at[sl, s, j], sem.at[sl])
                if start:
                    cp.start(priority=s)
                else:
                    cp.wait()
            return c
        lax.fori_loop(0, tc, body, 0, unroll=8)

    @pl.when(i == 0)
    def _():
        copies(0, 0, True)

    @pl.when(i + 1 < nt)
    def _():
        copies(i + 1, 1 - slot, True)

    copies(i, slot, False)
    route = route_ref[...]
    y0 = gbuf[slot, 0].reshape(tc, d)
    y1 = gbuf[slot, 1].reshape(tc, d)
    o_ref[...] = route[:, ROUTE_W1:ROUTE_W1 + 1] * y0 + route[:, ROUTE_W2:ROUTE_W2 + 1] * y1


def _moe_combine(ys, pos, route):
    n = route.shape[0]
    d = V7X_SUBLANES * V7X_LANES
    tc = min(n, MOE_COMBINE_TOKENS)
    grid_spec = pltpu.PrefetchScalarGridSpec(
        num_scalar_prefetch=1, grid=(n // tc,),
        in_specs=[pl.BlockSpec(memory_space=pl.ANY),
                  pl.BlockSpec((tc, ROUTER_LANES), lambda i, p: (i, 0))],
        out_specs=pl.BlockSpec((tc, d), lambda i, p: (i, 0)),
        scratch_shapes=[pltpu.VMEM((2, 2, tc, V7X_SUBLANES, V7X_LANES), F32), pltpu.SemaphoreType.DMA((2,))])
    return pl.pallas_call(
        functools.partial(_moe_combine_kernel, tc, n, d),
        out_shape=jax.ShapeDtypeStruct((n, d), F32),
        grid_spec=grid_spec, compiler_params=_params(("arbitrary",)), name="moe_combine")(pos, ys, route)


def _moe_routed(h2t, route, counts, wg, wu, wd):
    n = h2t.shape[0]
    assert h2t.shape[1:] == (V7X_SUBLANES, V7X_LANES)
    tm = MOE_ROW_TILE if n >= 32 * MOE_ROW_TILE else MOE_ROW_TILE_SMALL
    pos, tile, expert, lo, hi, first = _moe_tables(route, counts, tm)
    xs = _moe_scatter(h2t, pos)
    ys = _moe_experts(xs, tile, expert, lo, hi, first, wg, wu, wd, tm)
    return _moe_combine(ys, pos, route)


def _final_kernel(bb, tm, d, xa_ref, moe_ref, gtp_ref, g_ref, o_ref):
    rows = bb * tm
    x = xa_ref[...].reshape(rows, d) + _per_seq(gtp_ref, bb, tm, d) * moe_ref[...].reshape(rows, d)
    o_ref[...] = _rmsnorm(x, g_ref[...]).reshape(bb, tm, d)


def _final_norm(xa, res, g):
    bsz, t, d = xa.shape
    bb, tm = _row_tiling(bsz, t)
    tile = pl.BlockSpec((bb, tm, d), lambda b, i: (b, i, 0))
    res_ins, res_specs = _moe_residual_operands(res, bb, tm)
    return pl.pallas_call(
        functools.partial(_final_kernel, bb, tm, d),
        out_shape=jax.ShapeDtypeStruct((bsz, t, d), F32),
        grid=(bsz // bb, t // tm),
        in_specs=[tile] + res_specs + [_full_spec(g)],
        out_specs=tile,
        compiler_params=_params(("arbitrary", "arbitrary")),
        name="final_norm",
    )(xa, *res_ins, g)


def _to_lanes(z, n_heads):
    bsz, t, _ = z.shape
    return z.reshape(bsz, t, n_heads, HEAD_DIM).transpose(1, 3, 0, 2).reshape(t, HEAD_DIM, bsz * n_heads)


def _from_lanes(z, bsz, n_heads):
    t = z.shape[0]
    return z.reshape(t, HEAD_DIM, bsz, n_heads).transpose(2, 0, 3, 1).reshape(bsz, t, n_heads * HEAD_DIM)


def _head_param(p, n_heads):
    return jnp.tile(p.reshape(n_heads, HEAD_DIM).T, (1, V7X_LANES // n_heads))


def _trunk(x, c, shift0, conv0, wkv0, W):
    bsz, t, d = x.shape
    n_heads = d // HEAD_DIM
    depth = W["w_in_bf"].shape[0]
    mod = _modulation(c, W["ada_w"], W["ada_b"])
    bufs, shifts, states = [], [], []
    res = None
    xa = x
    for l in range(depth):
        sh1, sc1, gt1, sh2, sc2, gt2 = (mod[l, :, i * d:(i + 1) * d][:, None, :] for i in range(6))
        row = lambda p: p[l][None, :]
        pre = _mixer_pre(
            xa, res, sh1, sc1, row(W["norm_mix_g"]), W["w_in_bf"][l], shift0[l][:, None, :], conv0[l],
            row(W["mu_shift"]), W["conv_w"][l], row(W["conv_b"]), row(W["conv_ln_g"]), row(W["conv_ln_b"]),
            row(W["rw_w0"]), W["lora_bf"][l], row(W["rw_a0"]), W["g_up_bf"][l])
        if res is not None:
            xa, *pre = pre
        yag, gb, r, lw, k, v, a, convn, shiftn = pre
        head_params = (W["rw_k_k"][l], W["rw_k_a"][l], W["rw_r_k"][l].reshape(-1), W["rw_lnx_g"][l], W["rw_lnx_b"][l])
        if wkv0 is None:
            y, s_l = _wkv_chunked(r, lw, k, v, a, *(p[None, :] for p in head_params))
            states.append(s_l)
        else:
            s0 = wkv0[l].transpose(2, 3, 0, 1).reshape(HEAD_DIM, HEAD_DIM, bsz * n_heads)
            y_l, s_l = _wkv_scan(*(_to_lanes(z, n_heads) for z in (r, lw, k, v, a)),
                                 *(_head_param(p, n_heads) for p in head_params), s0)
            y = _from_lanes(y_l, bsz, n_heads)
            states.append(s_l.reshape(HEAD_DIM, HEAD_DIM, bsz, n_heads).transpose(2, 3, 0, 1))
        x1, h2t, route, counts = _mixer_post(xa, yag, gb, y, gt1, sh2, sc2, row(W["norm_ffn_g"]), W["w_out_bf"][l],
                                             W["wr_hi"][l], W["wr_lo"][l], W["br"][l])
        moe = _moe_routed(h2t.reshape(bsz * t, V7X_SUBLANES, V7X_LANES), route.reshape(bsz * t, ROUTER_LANES),
                          counts[0], W["moe_w_gate"][l], W["moe_w_up"][l], W["moe_w_down"][l])
        xa, res = x1, (moe.reshape(bsz, t, d), gt2)
        bufs.append(convn)
        shifts.append(shiftn[:, 0, :])
    y_out = _final_norm(xa, res, W["final_norm_g"][None, :])
    return y_out, jnp.stack(bufs), jnp.stack(shifts), jnp.stack(states)


def kernel(x_prompt, x_sample, state_conv, state_shift, state_wkv, c_prompt, c_sample, norm_mix_g, norm_ffn_g, final_norm_g, ada_w, ada_b, w_in, mu_shift, conv_w, conv_b, conv_ln_g, conv_ln_b, rw_w0, rw_w_up, rw_a0, rw_a_up, rw_g_up, rw_k_k, rw_k_a, rw_r_k, rw_lnx_g, rw_lnx_b, w_out, moe_w_group, moe_b_group, moe_w_expert, moe_b_expert, moe_w_gate, moe_w_up, moe_w_down):
    depth, d, _ = w_in.shape
    zw = jnp.zeros_like(rw_w_up)
    lora = jnp.concatenate([jnp.concatenate([rw_w_up, zw], axis=2), jnp.concatenate([zw, rw_a_up], axis=2)], axis=1)
    pad = ROUTER_LANES - N_EXPERTS - N_GROUPS
    wr = jnp.concatenate([moe_w_expert, moe_w_group, jnp.zeros((depth, d, pad), F32)], axis=2)
    wr_hi = wr.astype(BF16)
    br = jnp.concatenate([moe_b_expert, moe_b_group, jnp.zeros((depth, pad), F32)], axis=1)[:, None, :]
    W = dict(norm_mix_g=norm_mix_g, norm_ffn_g=norm_ffn_g, final_norm_g=final_norm_g, ada_w=ada_w, ada_b=ada_b,
             w_in_bf=w_in.astype(BF16), mu_shift=mu_shift, conv_w=conv_w, conv_b=conv_b, conv_ln_g=conv_ln_g,
             conv_ln_b=conv_ln_b, rw_w0=rw_w0, lora_bf=lora.astype(BF16), rw_a0=rw_a0, g_up_bf=rw_g_up.astype(BF16),
             rw_k_k=rw_k_k, rw_k_a=rw_k_a, rw_r_k=rw_r_k, rw_lnx_g=rw_lnx_g, rw_lnx_b=rw_lnx_b,
             w_out_bf=w_out.astype(BF16), wr_hi=wr_hi, wr_lo=(wr - wr_hi.astype(F32)).astype(BF16), br=br,
             moe_w_gate=moe_w_gate, moe_w_up=moe_w_up, moe_w_down=moe_w_down)
    bp = x_prompt.shape[0]
    conv0 = jnp.zeros((depth, bp, CONV_BUF, d), x_prompt.dtype)
    shift0 = jnp.zeros((depth, bp, d), x_prompt.dtype)
    y_p, conv_p, shift_p, wkv_p = _trunk(x_prompt, c_prompt, shift0, conv0, None, W)
    y_s, conv_s, shift_s, wkv_s = _trunk(x_sample, c_sample, state_shift, state_conv, state_wkv, W)
    return (y_p, y_s, conv_p, shift_p, wkv_p.astype(state_wkv.dtype), conv_s, shift_s, wkv_s.astype(state_wkv.dtype))
```

```python
import functools

import jax
import jax.numpy as jnp
from jax import lax
from jax.experimental import pallas as pl
from jax.experimental.pallas import tpu as pltpu

F32 = jnp.float32
BF16 = jnp.bfloat16

HEAD_DIM = 64
CONV_WIDTH = 31
CONV_BUF = CONV_WIDTH - 1
LORA_W = 64
LORA_A = 64
N_GROUPS = 4
EXPERTS_PER_GROUP = 8
N_EXPERTS = N_GROUPS * EXPERTS_PER_GROUP
RMS_EPS = 1e-6
LN_EPS = 1e-5
GN_EPS = 64e-5

V7X_LANES = 128
V7X_SUBLANES = 8
V7X_VMEM_LIMIT_BYTES = 56 * 1024 * 1024

ROW_TILE = 256
SHORT_SEQ_ROWS = 128
HIST_ROWS = 32
HIST_PAD = HIST_ROWS - CONV_BUF
CONV_LANE_CHUNK = 256
WKV_CHUNK = 64
WKV_GROUP_HEADS = 4
WKV_GROUP = WKV_GROUP_HEADS * HEAD_DIM
WKV_INTERLEAVE_PER_MATMUL = 2
ROUTER_LANES = V7X_LANES
ROUTE_E1, ROUTE_E2, ROUTE_W1, ROUTE_W2, ROUTE_R1, ROUTE_R2 = 0, 1, 2, 3, 4, 5
MOE_ROW_TILE = 256
MOE_ROW_TILE_SMALL = 64
MOE_SCATTER_TOKENS = 256
MOE_COMBINE_TOKENS = 128


def _dot(a, b):
    return jnp.dot(a, b, preferred_element_type=F32)


def _sigmoid(x):
    return 1.0 / (1.0 + jnp.exp(-x))


def _rmsnorm(x, g):
    return x * lax.rsqrt(jnp.mean(x * x, axis=-1, keepdims=True) + RMS_EPS) * g


def _params(sem):
    return pltpu.CompilerParams(dimension_semantics=sem, vmem_limit_bytes=V7X_VMEM_LIMIT_BYTES)


def _row_tiling(bsz, t):
    tm = min(t, ROW_TILE)
    bb = max(1, min(bsz, SHORT_SEQ_ROWS // tm))
    assert t % tm == 0 and bsz % bb == 0 and tm % V7X_SUBLANES == 0
    assert bb == 1 or tm == t
    return bb, tm


def _per_seq(ref, bb, n, d):
    p = ref[...]
    if bb == 1:
        return p[0]
    return jnp.broadcast_to(p, (bb, n, d)).reshape(bb * n, d)


def _full_spec(a):
    return pl.BlockSpec(a.shape, lambda b, i: (0,) * a.ndim)


def _moe_residual_operands(res, bb, tm):
    moe, gate = res
    d = moe.shape[-1]
    return ([moe, gate], [pl.BlockSpec((bb, tm, d), lambda b, i: (b, i, 0)),
                          pl.BlockSpec((bb, 1, d), lambda b, i: (b, 0, 0))])


def _ada_kernel(c_ref, w_ref, b_ref, o_ref):
    c = c_ref[...]
    s = c * _sigmoid(c)
    o_ref[0] = _dot(s.astype(BF16), w_ref[0].astype(BF16)) + b_ref[0]


def _modulation(c, ada_w, ada_b):
    depth, d, n = ada_w.shape
    bsz = c.shape[0]
    tn = 1536
    return pl.pallas_call(
        _ada_kernel,
        out_shape=jax.ShapeDtypeStruct((depth, bsz, n), F32),
        grid=(depth, n // tn),
        in_specs=[
            pl.BlockSpec((bsz, d), lambda l, j: (0, 0)),
            pl.BlockSpec((1, d, tn), lambda l, j: (l, 0, j)),
            pl.BlockSpec((1, 1, tn), lambda l, j: (l, 0, j)),
        ],
        out_specs=pl.BlockSpec((1, bsz, tn), lambda l, j: (l, 0, j)),
        compiler_params=_params(("arbitrary", "arbitrary")),
        name="ada_modulation",
    )(c, ada_w, ada_b.reshape(depth, 1, n))


def _mixer_pre_kernel(has_res, fuse_wkv, bb, tm, d, *refs):
    if has_res:
        xa_ref, moe_ref, gtp_ref, *refs = refs
    else:
        xa_ref, *refs = refs
    (sh_ref, sc_ref, g_ref, win_ref, shift0_ref, conv0_ref, mu_ref, cw_ref, cb_ref, clg_ref, clb_ref,
     w0_ref, lora_ref, a0_ref, gup_ref, *refs) = refs
    if fuse_wkv:
        *wkv_params, ones_ref, tri_ref = refs[:7]
        refs = refs[7:]
    if has_res:
        xres_ref, *refs = refs
    if fuse_wkv:
        (yag_ref, gb_ref, y_ref, convn_ref, shiftn_ref, sout_ref,
         ubuf_ref, z_ref, carry_ref, shifted_ref, s_ref) = refs
    else:
        (yag_ref, gb_ref, r_ref, lw_ref, k_ref, v_ref, a_ref, convn_ref, shiftn_ref,
         ubuf_ref, z_ref, carry_ref, shifted_ref) = refs
    c_plain = 4 * d
    rows = bb * tm
    t = pl.program_id(1)
    nt = pl.num_programs(1)

    @pl.when(t == 0)
    def _():
        hp = _per_seq(shift0_ref, bb, V7X_SUBLANES, d)
        hp = jnp.broadcast_to(hp, (bb * V7X_SUBLANES, d)).astype(BF16)
        carry_ref[...] = _dot(hp, win_ref[:, c_plain:])
        ubuf_ref[:, pl.ds(0, HIST_PAD), :] = jnp.zeros((bb, HIST_PAD, d), F32)
        ubuf_ref[:, pl.ds(HIST_PAD, CONV_BUF), :] = conv0_ref[...]
        if fuse_wkv:
            s_ref[...] = jnp.zeros_like(s_ref)

    x = xa_ref[...].reshape(rows, d)
    if has_res:
        x = x + _per_seq(gtp_ref, bb, tm, d) * moe_ref[...].reshape(rows, d)
        xres_ref[...] = x.reshape(bb, tm, d)
    h = _rmsnorm(x, g_ref[...]) * (1.0 + _per_seq(sc_ref, bb, tm, d)) + _per_seq(sh_ref, bb, tm, d)
    shiftn_ref[...] = h.reshape(bb, tm, d)[:, tm - 1:tm, :]
    hb = h.astype(BF16)
    cur = _dot(hb, win_ref[:, c_plain:])
    proj = _dot(hb, win_ref[:, :c_plain])

    prev = pltpu.roll(cur, 1, 0)
    row = lax.broadcasted_iota(jnp.int32, cur.shape, 0)
    if bb == 1:
        prev = jnp.where(row == 0, carry_ref[pl.ds(0, 1), :], prev)
        carry_ref[pl.ds(0, 1), :] = cur[tm - 1:tm, :]
    else:
        prev = jnp.where((row & (tm - 1)) == 0, carry_ref[...], prev)
    rw = cur + (prev - cur) * mu_ref[...]

    r = rw[:, 0:d]
    k = rw[:, d:2 * d]
    v = rw[:, 2 * d:3 * d]
    xwa = rw[:, 3 * d:3 * d + LORA_W + LORA_A]
    xg = rw[:, 3 * d + LORA_W + LORA_A:]
    lane = lax.broadcasted_iota(jnp.int32, xwa.shape, 1)
    lhs = jnp.where(lane < LORA_W, jnp.tanh(xwa), xwa)
    za = _dot(lhs.astype(BF16), lora_ref[...])
    log_decay = -jnp.exp(-0.5) * _sigmoid(w0_ref[...] + za[:, 0:d])
    a = _sigmoid(a0_ref[...] + za[:, d:])
    gate = _dot(_sigmoid(xg).astype(BF16), gup_ref[...])
    blk = (bb, tm, d)

    u = proj[:, 0:d] * _sigmoid(proj[:, d:2 * d])
    ubuf_ref[:, pl.ds(HIST_ROWS, tm), :] = u.reshape(blk)
    rb = min(tm, 32)
    lc = CONV_LANE_CHUNK
    span = tm + HIST_ROWS - V7X_SUBLANES

    def conv_shift(b, c0):
        for s in range(1, V7X_SUBLANES):
            shifted_ref[s - 1] = ubuf_ref[b, pl.ds(s, span), c0:c0 + lc]

    def conv_rows(b, c0, r0):
        acc = jnp.broadcast_to(cb_ref[:, c0:c0 + lc], (rb, lc))
        for tap in range(CONV_WIDTH):
            s = (HIST_PAD + tap) % V7X_SUBLANES
            base = r0 + HIST_PAD + tap - s
            if s == 0:
                src = ubuf_ref[b, pl.ds(base, rb), c0:c0 + lc]
            else:
                src = shifted_ref[s - 1, pl.ds(base, rb), :]
            acc = acc + cw_ref[pl.ds(tap, 1), c0:c0 + lc] * src
        z_ref[pl.ds(b * tm + r0, rb), c0:c0 + lc] = acc

    conv_work = []
    for c0 in range(0, d, lc):
        for b in range(bb):
            conv_work.append(functools.partial(conv_shift, b, c0))
            conv_work += [functools.partial(conv_rows, b, c0, r0) for r0 in range(0, tm, rb)]

    if fuse_wkv:
        y_ref[...] = _wkv_chunk_math(tm, d, r, log_decay, k, v, a, wkv_params, ones_ref, tri_ref, s_ref,
                                     conv_work).reshape(blk)
    else:
        r_ref[...] = r.reshape(blk)
        k_ref[...] = k.reshape(blk)
        v_ref[...] = v.reshape(blk)
        lw_ref[...] = log_decay.reshape(blk)
        a_ref[...] = a.reshape(blk)
    gb_ref[...] = (_sigmoid(proj[:, 3 * d:4 * d]) * gate).reshape(blk)

    for work in conv_work:
        work()
    z = z_ref[...]
    mu = jnp.mean(z, axis=-1, keepdims=True)
    zc = z - mu
    var = jnp.mean(zc * zc, axis=-1, keepdims=True)
    zn = zc * lax.rsqrt(var + LN_EPS) * clg_ref[...] + clb_ref[...]
    yag_ref[...] = (_sigmoid(proj[:, 2 * d:3 * d]) * (zn * _sigmoid(zn))).reshape(blk)

    convn_ref[...] = ubuf_ref[:, pl.ds(tm + HIST_PAD, CONV_BUF), :]
    if tm >= HIST_ROWS:
        ubuf_ref[:, pl.ds(0, HIST_ROWS), :] = ubuf_ref[:, pl.ds(tm, HIST_ROWS), :]

    if fuse_wkv:
        @pl.when(t == nt - 1)
        def _():
            for hd in range(d // HEAD_DIM):
                g, hs = divmod(hd, WKV_GROUP_HEADS)
                hs = slice(hs * HEAD_DIM, (hs + 1) * HEAD_DIM)
                sout_ref[0, hd] = s_ref[g, hs, hs]


def _mixer_pre(xa, res, sh, sc, g, win_bf, layer, shift0, conv0, mu, cw, cb, clg, clb, w0, lora_bf, a0, gup_bf,
               wkv_params=None):
    bsz, t, d = xa.shape
    bb, tm = _row_tiling(bsz, t)
    assert t == tm or tm >= HIST_ROWS
    assert bb == 1 or tm == V7X_SUBLANES
    fuse_wkv = wkv_params is not None
    c_in = win_bf.shape[2]
    c_rwkv = c_in - 4 * d
    tile = pl.BlockSpec((bb, tm, d), lambda b, i: (b, i, 0))
    perb = pl.BlockSpec((bb, 1, d), lambda b, i: (b, 0, 0))
    hist = pl.BlockSpec((bb, CONV_BUF, d), lambda b, i: (b, 0, 0))
    ins, specs = [xa], [tile]
    n_act = 3 if fuse_wkv else 7
    if res is not None:
        res_ins, res_specs = _moe_residual_operands(res, bb, tm)
        ins += res_ins
        specs += res_specs
        n_act += 1
    ins += [sh, sc, g, win_bf, shift0, conv0, mu, cw, cb, clg, clb, w0, lora_bf, a0, gup_bf]
    specs += [perb, perb, _full_spec(g),
              pl.BlockSpec((None, d, c_in), lambda b, i: (layer, 0, 0), pipeline_mode=pl.Buffered(1)),
              perb, hist] + [_full_spec(p) for p in (mu, cw, cb, clg, clb, w0, lora_bf, a0, gup_bf)]
    act = jax.ShapeDtypeStruct((bsz, t, d), F32)
    out_shape = [act] * n_act + [jax.ShapeDtypeStruct((bsz, CONV_BUF, d), F32), jax.ShapeDtypeStruct((bsz, 1, d), F32)]
    out_specs = [tile] * n_act + [hist, perb]
    scratch = [pltpu.VMEM((bb, tm + HIST_ROWS, d), F32), pltpu.VMEM((bb * tm, d), F32),
               pltpu.VMEM((bb * V7X_SUBLANES, c_rwkv), F32),
               pltpu.VMEM((V7X_SUBLANES - 1, tm + HIST_ROWS - V7X_SUBLANES, CONV_LANE_CHUNK), F32)]
    if fuse_wkv:
        assert bb == 1 and tm % WKV_CHUNK == 0 and d % WKV_GROUP == 0
        n_heads = d // HEAD_DIM
        consts = _wkv_constants()
        ins += list(wkv_params) + list(consts)
        specs += [_full_spec(p) for p in wkv_params] + [_full_spec(c) for c in consts]
        out_shape.append(jax.ShapeDtypeStruct((bsz, n_heads, HEAD_DIM, HEAD_DIM), F32))
        out_specs.append(pl.BlockSpec((1, n_heads, HEAD_DIM, HEAD_DIM), lambda b, i: (b, 0, 0, 0)))
        scratch.append(pltpu.VMEM((d // WKV_GROUP, WKV_GROUP, WKV_GROUP), F32))
    return pl.pallas_call(
        functools.partial(_mixer_pre_kernel, res is not None, fuse_wkv, bb, tm, d),
        out_shape=out_shape,
        grid=(bsz // bb, t // tm),
        in_specs=specs,
        out_specs=out_specs,
        scratch_shapes=scratch,
        compiler_params=_params(("arbitrary", "arbitrary")),
        name="mixer_pre",
    )(*ins)


def _wkv_kernel(tc, r_ref, lw_ref, k_ref, v_ref, a_ref, kkp_ref, kap_ref, rkp_ref, lng_ref, lnb_ref, s0_ref,
                y_ref, s_ref, o_ref):
    n = HEAD_DIM

    @pl.when(pl.program_id(1) == 0)
    def _():
        s_ref[...] = s0_ref[...]

    def step(tt, carry):
        r_t = r_ref[tt]
        k_t = k_ref[tt]
        a_t = a_ref[tt]
        w_t = jnp.exp(lw_ref[tt])
        kk = k_t * kkp_ref[...]
        kk = kk * lax.rsqrt(jnp.sum(kk * kk, axis=0, keepdims=True) + 1e-12)
        b_t = kk * a_t
        k2 = k_t * (1.0 + (a_t - 1.0) * kap_ref[...])

        def row(i, c):
            si = s_ref[i]
            sa = jnp.sum(si * kk, axis=0, keepdims=True)
            vi = v_ref[tt, pl.ds(i, 1), :]
            sn = si * w_t - sa * b_t + vi * k2
            s_ref[i] = sn
            o_ref[pl.ds(i, 1), :] = jnp.sum(sn * r_t, axis=0, keepdims=True)
            return c

        lax.fori_loop(0, n, row, 0, unroll=8)
        o = o_ref[...]
        mu = jnp.mean(o, axis=0, keepdims=True)
        oc = o - mu
        var = jnp.mean(oc * oc, axis=0, keepdims=True)
        yn = oc * lax.rsqrt(var + GN_EPS) * lng_ref[...] + lnb_ref[...]
        bonus = jnp.sum(r_t * k2 * rkp_ref[...], axis=0, keepdims=True) * v_ref[tt]
        y_ref[tt] = yn + bonus
        return carry

    lax.fori_loop(0, tc, step, 0)


def _wkv_scan(r, w, k, v, a, kkp, kap, rkp, lng, lnb, s0):
    t, n, nl = r.shape
    tc = min(t, 32)
    lb = V7X_LANES
    seq = pl.BlockSpec((tc, n, lb), lambda l, i: (i, 0, l))
    par = pl.BlockSpec((n, lb), lambda l, i: (0, 0))
    st = pl.BlockSpec((n, n, lb), lambda l, i: (0, 0, l))
    return pl.pallas_call(
        functools.partial(_wkv_kernel, tc),
        out_shape=[jax.ShapeDtypeStruct((t, n, nl), F32), jax.ShapeDtypeStruct((n, n, nl), F32)],
        grid=(nl // lb, t // tc),
        in_specs=[seq] * 5 + [par] * 5 + [st],
        out_specs=[seq, st],
        scratch_shapes=[pltpu.VMEM((n, lb), F32)],
        compiler_params=_params(("arbitrary", "arbitrary")),
        name="wkv_scan",
    )(r, w, k, v, a, kkp, kap, rkp, lng, lnb, s0)


def _bmm(a, b):
    return jnp.einsum("gck,gkn->gcn", a, b, preferred_element_type=F32)


def _bmm_nt(a, b):
    return jnp.einsum("gck,gnk->gcn", a, b, preferred_element_type=F32)


def _bmm_tn(a, b):
    return jnp.einsum("gtc,gtn->gcn", a, b, preferred_element_type=F32)


def _split_bf16(x):
    hi = x.astype(BF16)
    return hi, (x - hi.astype(F32)).astype(BF16)


def _wkv_constants():
    li = lax.broadcasted_iota(jnp.int32, (WKV_GROUP, WKV_GROUP), 0) // HEAD_DIM
    lj = lax.broadcasted_iota(jnp.int32, (WKV_GROUP, WKV_GROUP), 1) // HEAD_DIM
    ti = lax.broadcasted_iota(jnp.int32, (WKV_CHUNK, WKV_CHUNK), 0)
    tj = lax.broadcasted_iota(jnp.int32, (WKV_CHUNK, WKV_CHUNK), 1)
    return (li == lj).astype(BF16), (ti >= tj).astype(BF16)


def _wkv_chunk_math(tt, d, r, lw, k, v, a, params, ones_ref, tri_ref, s_ref, interleaved):
    c = WKV_CHUNK
    gw = WKV_GROUP
    ng = d // gw
    nc = tt // c
    n_items = nc * ng
    kkp_ref, kap_ref, rkp_ref, lng_ref, lnb_ref = params

    lane = lax.broadcasted_iota(jnp.int32, (1, c, gw), 2)
    rowi = lax.broadcasted_iota(jnp.int32, (1, c, gw), 1)
    head_of_lane = lax.shift_right_logical(lane, 6)
    hmask = [head_of_lane == h for h in range(WKV_GROUP_HEADS)]
    tcol = lane & (c - 1)
    strict = tcol < rowi
    incl = tcol <= rowi
    bl = lax.broadcasted_iota(jnp.int32, (1, gw, gw), 2)
    br = lax.broadcasted_iota(jnp.int32, (1, gw, gw), 1)
    blockmask = lax.shift_right_logical(bl, 6) == lax.shift_right_logical(br, 6)

    def items(x):
        x = x.reshape(nc, c, d)
        return jnp.stack([x[:, :, g * gw:(g + 1) * gw] for g in range(ng)], axis=1).reshape(n_items, c, gw)

    def per_group(ref):
        p = jnp.stack([ref[:, g * gw:(g + 1) * gw] for g in range(ng)], axis=0)
        return jnp.broadcast_to(p[None], (nc, ng, 1, gw)).reshape(n_items, 1, gw)

    def bd(xf):
        return jnp.concatenate([jnp.where(hmask[h], xf, 0.0) for h in range(WKV_GROUP_HEADS)], axis=1).astype(BF16)

    def segsum(x):
        shp = x.shape
        hi, lo = _split_bf16(x.reshape(-1, gw))
        return (_dot(hi, ones_ref[...]) + _dot(lo, ones_ref[...])).reshape(shp)

    def filled(y):
        for _ in range(WKV_INTERLEAVE_PER_MATMUL):
            if interleaved:
                interleaved.pop(0)()
        return y

    def bmm(x, y):
        return filled(_bmm(x, y))

    def bmm_nt(x, y):
        return filled(_bmm_nt(x, y))

    def bmm_tn(x, y):
        return filled(_bmm_tn(x, y))

    r, lw, k, v, a = (items(x) for x in (r, lw, k, v, a))
    kk = k * per_group(kkp_ref)
    kk = kk * lax.rsqrt(segsum(kk * kk) + 1e-12)
    b = kk * a
    k2 = k * (1.0 + (a - 1.0) * per_group(kap_ref))
    lw_hi, lw_lo = _split_bf16(lw)
    tri = jnp.broadcast_to(tri_ref[...][None], (n_items, c, c))
    cum = bmm(tri, lw_hi) + bmm(tri, lw_lo)
    cl = cum[:, c - 1:c, :]
    at = -(kk * jnp.exp(cum - lw))
    rt = r * jnp.exp(cum)
    winv = jnp.exp(-cum)
    wrem = jnp.exp(cl - cum)
    wc = jnp.exp(cl)
    lhs = jnp.concatenate([at, rt], axis=1).astype(BF16)
    rhs = jnp.concatenate([bd(b * winv), bd(k2 * winv)], axis=1)
    sc = bmm_nt(lhs, rhs)
    l_ab = jnp.where(strict, sc[:, 0:c, 0:gw], 0.0)
    l_ak = jnp.where(strict, sc[:, 0:c, gw:], 0.0)
    l_rb = jnp.where(incl, sc[:, c:, 0:gw], 0.0)
    l_rk = jnp.where(incl, sc[:, c:, gw:], 0.0)
    pk = l_ab
    tm = l_ab
    pbd = bd(pk)
    for _ in range(c.bit_length() - 2):
        pk = bmm(pk.astype(BF16), pbd)
        pbd = bd(pk)
        tm = tm + pk + bmm(tm.astype(BF16), pbd)
    tmb = tm.astype(BF16)
    vbd = bd(v)
    ah = at + bmm(tmb, bd(at))
    x1 = bmm(l_ak.astype(BF16), vbd)
    gm = x1 + bmm(tmb, bd(x1))
    l_rbb = l_rb.astype(BF16)
    rh = (rt + bmm(l_rbb, bd(ah))).astype(BF16)
    o_ind = bmm(jnp.concatenate([l_rbb, l_rk.astype(BF16)], axis=2), jnp.concatenate([bd(gm), vbd], axis=1))
    bb = (b * wrem).astype(BF16)
    kb = (k2 * wrem).astype(BF16)
    m = jnp.where(blockmask, bmm_tn(ah.astype(BF16), bb), 0.0).astype(BF16)
    n = jnp.where(blockmask, bmm_tn(jnp.concatenate([gm, v], axis=1).astype(BF16),
                                    jnp.concatenate([bb, kb], axis=1)), 0.0)

    s = s_ref[...]
    outs = []
    for ci in range(nc):
        sl = slice(ci * ng, (ci + 1) * ng)
        sb = s.astype(BF16)
        outs.append(bmm_nt(rh[sl], sb) + o_ind[sl])
        s = s * wc[sl] + bmm(sb, m[sl]) + n[sl]
    s_ref[...] = s

    inv_n = 1.0 / HEAD_DIM
    o = jnp.stack(outs, axis=0).reshape(n_items, c, gw)
    mu = segsum(o) * inv_n
    oc = o - mu
    var = segsum(oc * oc) * inv_n
    yn = oc * lax.rsqrt(var + GN_EPS) * per_group(lng_ref) + per_group(lnb_ref)
    bonus = segsum(r * k2 * per_group(rkp_ref)) * v
    y = (yn + bonus).reshape(nc, ng, c, gw)
    return jnp.concatenate([y[:, g].reshape(tt, gw) for g in range(ng)], axis=1)


def _mixer_post_kernel(bb, tm, d, x_ref, yag_ref, gb_ref, y_ref, gt1_ref, sh_ref, sc_ref, g_ref, wout_ref,
                       wrh_ref, wrl_ref, br_ref, tri_ref, x1_ref, h2_ref, route_ref, counts_ref, cnt_ref):
    rows = bb * tm

    @pl.when((pl.program_id(0) == 0) & (pl.program_id(1) == 0))
    def _():
        cnt_ref[...] = jnp.zeros_like(cnt_ref)

    x = x_ref[...].reshape(rows, d)
    merged = (yag_ref[...] + gb_ref[...] * y_ref[...]).reshape(rows, d)
    x1 = x + _per_seq(gt1_ref, bb, tm, d) * _dot(merged.astype(BF16), wout_ref[...])
    x1_ref[...] = x1.reshape(bb, tm, d)
    h2 = _rmsnorm(x1, g_ref[...]) * (1.0 + _per_seq(sc_ref, bb, tm, d)) + _per_seq(sh_ref, bb, tm, d)
    h2_ref[...] = h2.reshape(bb, tm, V7X_SUBLANES, d // V7X_SUBLANES)
    h_hi = h2.astype(BF16)
    h_lo = (h2 - h_hi.astype(F32)).astype(BF16)
    logits = _dot(h_hi, wrh_ref[...]) + (_dot(h_lo, wrh_ref[...]) + _dot(h_hi, wrl_ref[...])) + br_ref[...]

    lane_i = lax.broadcasted_iota(jnp.int32, logits.shape, 1)
    lane = lane_i.astype(F32)
    grp_of_lane = lax.shift_right_logical(lane_i, 3).astype(F32)
    neg = -jnp.inf
    none = 1e9
    is_g = (lane_i >= N_EXPERTS) & (lane_i < N_EXPERTS + N_GROUPS)
    glv = jnp.where(is_g, logits, neg)
    gmax = jnp.max(glv, axis=-1, keepdims=True)
    gidx = jnp.min(jnp.where(is_g & (glv == gmax), lane - N_EXPERTS, none), axis=-1, keepdims=True)
    pg = 1.0 / jnp.sum(jnp.where(is_g, jnp.exp(glv - gmax), 0.0), axis=-1, keepdims=True)
    in_grp = (lane_i < N_EXPERTS) & (grp_of_lane == gidx)
    ev = jnp.where(in_grp, logits, neg)
    v1 = jnp.max(ev, axis=-1, keepdims=True)
    i1 = jnp.min(jnp.where(in_grp & (ev == v1), lane, none), axis=-1, keepdims=True)
    ev2 = jnp.where(lane == i1, neg, ev)
    v2 = jnp.max(ev2, axis=-1, keepdims=True)
    i2 = jnp.min(jnp.where(in_grp & (ev2 == v2) & (lane != i1), lane, none), axis=-1, keepdims=True)
    e2 = jnp.exp(v2 - v1)
    den = 1.0 + e2
    hit = ((lane == i1) | (lane == i2)).astype(F32)
    before = cnt_ref[pl.ds(0, 1), :] + _dot(tri_ref[...], hit.astype(BF16))
    r1 = jnp.sum(jnp.where(lane == i1, before, 0.0), axis=-1, keepdims=True)
    r2 = jnp.sum(jnp.where(lane == i2, before, 0.0), axis=-1, keepdims=True)
    counts = cnt_ref[pl.ds(0, 1), :] + jnp.sum(hit, axis=0, keepdims=True)
    cnt_ref[pl.ds(0, 1), :] = counts
    counts_ref[...] = jnp.broadcast_to(counts, counts_ref.shape)
    route = (jnp.where(lane_i == ROUTE_E1, i1, 0.0) + jnp.where(lane_i == ROUTE_E2, i2, 0.0)
             + jnp.where(lane_i == ROUTE_W1, (1.0 / den) * pg, 0.0) + jnp.where(lane_i == ROUTE_W2, (e2 / den) * pg, 0.0)
             + jnp.where(lane_i == ROUTE_R1, r1, 0.0) + jnp.where(lane_i == ROUTE_R2, r2, 0.0))
    route_ref[...] = route.reshape(bb, tm, ROUTER_LANES)


def _mixer_post(x, yag, gb, y, gt1, sh, sc, g, wout_bf, wr_hi, wr_lo, br):
    bsz, t, d = x.shape
    bb, tm = _row_tiling(bsz, t)
    rows = bb * tm
    tile = pl.BlockSpec((bb, tm, d), lambda b, i: (b, i, 0))
    perb = pl.BlockSpec((bb, 1, d), lambda b, i: (b, 0, 0))
    ri = lax.broadcasted_iota(jnp.int32, (rows, rows), 0)
    ci = lax.broadcasted_iota(jnp.int32, (rows, rows), 1)
    tri = (ci < ri).astype(BF16)
    ins = [x, yag, gb, y, gt1, sh, sc, g, wout_bf, wr_hi, wr_lo, br, tri]
    specs = [tile, tile, tile, tile, perb, perb, perb] + [_full_spec(p) for p in (g, wout_bf, wr_hi, wr_lo, br, tri)]
    token_tiles = (bb, tm, V7X_SUBLANES, d // V7X_SUBLANES)
    return pl.pallas_call(
        functools.partial(_mixer_post_kernel, bb, tm, d),
        out_shape=[jax.ShapeDtypeStruct((bsz, t, d), F32),
                   jax.ShapeDtypeStruct((bsz, t, V7X_SUBLANES, d // V7X_SUBLANES), F32),
                   jax.ShapeDtypeStruct((bsz, t, ROUTER_LANES), F32),
                   jax.ShapeDtypeStruct((V7X_SUBLANES, ROUTER_LANES), F32)],
        grid=(bsz // bb, t // tm),
        in_specs=specs,
        out_specs=[tile, pl.BlockSpec(token_tiles, lambda b, i: (b, i, 0, 0)),
                   pl.BlockSpec((bb, tm, ROUTER_LANES), lambda b, i: (b, i, 0)),
                   pl.BlockSpec((V7X_SUBLANES, ROUTER_LANES), lambda b, i: (0, 0))],
        scratch_shapes=[pltpu.VMEM((V7X_SUBLANES, ROUTER_LANES), F32)],
        compiler_params=_params(("arbitrary", "arbitrary")),
        name="mixer_post",
    )(*ins)


def _moe_tables(route, counts, tm):
    n = route.shape[0]
    nt = 2 * n // tm
    cnt = counts[:N_EXPERTS].astype(jnp.int32)
    ends = jnp.cumsum(cnt)
    starts = ends - cnt
    ids = jnp.arange(N_EXPERTS, dtype=jnp.int32)[None, :]

    def position(e_lane, r_lane):
        e = route[:, e_lane].astype(jnp.int32)
        return jnp.sum(jnp.where(e[:, None] == ids, starts[None, :], 0), axis=1) + route[:, r_lane].astype(jnp.int32)

    pos = jnp.concatenate([position(ROUTE_E1, ROUTE_R1), position(ROUTE_E2, ROUTE_R2)])
    bounds = jnp.sort(jnp.concatenate([jnp.arange(nt + 1, dtype=jnp.int32) * tm, ends]))
    lo_abs, hi_abs = bounds[:-1], bounds[1:]
    tile = jnp.minimum(lo_abs // tm, nt - 1)
    nonzero = hi_abs > lo_abs
    expert = jnp.minimum(jnp.sum((ends[None, :] <= lo_abs[:, None]).astype(jnp.int32), axis=1), N_EXPERTS - 1)
    expert = jnp.maximum(lax.cummax(jnp.where(nonzero, expert, -1)), 0)
    prev_tile = jnp.concatenate([jnp.full((1,), -1, jnp.int32), lax.cummax(jnp.where(nonzero, tile, -1))[:-1]])
    first = (nonzero & (tile > prev_tile)).astype(jnp.int32)
    return pos, tile, expert, lo_abs - tile * tm, hi_abs - tile * tm, first


def _moe_scatter_kernel(ts, n, pos_ref, h_ref, xs_hbm, sem):
    i = pl.program_id(0)

    def copies(start):
        def body(j, c):
            for s in range(2):
                cp = pltpu.make_async_copy(h_ref.at[j], xs_hbm.at[pos_ref[s * n + i * ts + j]], sem)
                if start:
                    cp.start(priority=s)
                else:
                    cp.wait()
            return c
        lax.fori_loop(0, ts, body, 0, unroll=8)

    copies(True)
    copies(False)


def _moe_scatter(h2t, pos):
    n = h2t.shape[0]
    ts = min(n, MOE_SCATTER_TOKENS)
    grid_spec = pltpu.PrefetchScalarGridSpec(
        num_scalar_prefetch=1, grid=(n // ts,),
        in_specs=[pl.BlockSpec((ts, V7X_SUBLANES, V7X_LANES), lambda i, p: (i, 0, 0))],
        out_specs=pl.BlockSpec(memory_space=pl.ANY),
        scratch_shapes=[pltpu.SemaphoreType.DMA(())])
    return pl.pallas_call(
        functools.partial(_moe_scatter_kernel, ts, n),
        out_shape=jax.ShapeDtypeStruct((2 * n, V7X_SUBLANES, V7X_LANES), F32),
        grid_spec=grid_spec, compiler_params=_params(("arbitrary",)), name="moe_scatter")(pos, h2t)


def _moe_experts_kernel(tm, d, tile_ref, exp_ref, lo_ref, hi_ref, first_ref, xs_ref, wg_ref, wu_ref, wd_ref, ys_ref):
    w = pl.program_id(0)
    lo = lo_ref[w]
    hi = hi_ref[w]

    @pl.when(hi > lo)
    def _():
        x = xs_ref[...].reshape(tm, d).astype(BF16)
        hg = _dot(x, wg_ref[0].astype(BF16))
        hu = _dot(x, wu_ref[0].astype(BF16))
        act = hg * _sigmoid(hg) * hu
        y = _dot(act.astype(BF16), wd_ref[0].astype(BF16))
        row = lax.broadcasted_iota(jnp.int32, (tm, 1), 0)
        y = jnp.where((row >= lo) & (row < hi), y, 0.0).reshape(tm, V7X_SUBLANES, V7X_LANES)

        @pl.when(first_ref[w] == 1)
        def _():
            ys_ref[...] = y

        @pl.when(first_ref[w] == 0)
        def _():
            ys_ref[...] += y


def _moe_experts(xs, tile, expert, lo, hi, first, wg, wu, wd, layer, tm):
    d = V7X_SUBLANES * V7X_LANES
    f = wg.shape[-1]
    row_tile = pl.BlockSpec((tm, V7X_SUBLANES, V7X_LANES), lambda w, t, e, *_: (t[w], 0, 0))
    grid_spec = pltpu.PrefetchScalarGridSpec(
        num_scalar_prefetch=5, grid=(tile.shape[0],),
        in_specs=[row_tile,
                  pl.BlockSpec((None, 1, d, f), lambda w, t, e, *_: (layer, e[w], 0, 0)),
                  pl.BlockSpec((None, 1, d, f), lambda w, t, e, *_: (layer, e[w], 0, 0)),
                  pl.BlockSpec((None, 1, f, d), lambda w, t, e, *_: (layer, e[w], 0, 0))],
        out_specs=row_tile)
    return pl.pallas_call(
        functools.partial(_moe_experts_kernel, tm, d),
        out_shape=jax.ShapeDtypeStruct(xs.shape, F32),
        grid_spec=grid_spec, compiler_params=_params(("arbitrary",)), name="moe_experts")(
            tile, expert, lo, hi, first, xs, wg, wu, wd)


def _moe_combine_kernel(tc, n, d, pos_ref, ys_hbm, route_ref, o_ref, gbuf, sem):
    i = pl.program_id(0)
    nt = pl.num_programs(0)
    slot = lax.rem(i, 2)

    def copies(blk, sl, start):
        def body(j, c):
            for s in range(2):
                cp = pltpu.make_async_copy(ys_hbm.at[pos_ref[s * n + blk * tc + j]], gbuf.at[sl, s, j], sem.at[sl])
                if start:
                    cp.start(priority=s)
                else:
                    cp.wait()
            return c
        lax.fori_loop(0, tc, body, 0, unroll=8)

    @pl.when(i == 0)
    def _():
        copies(0, 0, True)

    @pl.when(i + 1 < nt)
    def _():
        copies(i + 1, 1 - slot, True)

    copies(i, slot, False)
    route = route_ref[...]
    y0 = gbuf[slot, 0].reshape(tc, d)
    y1 = gbuf[slot, 1].reshape(tc, d)
    o_ref[...] = route[:, ROUTE_W1:ROUTE_W1 + 1] * y0 + route[:, ROUTE_W2:ROUTE_W2 + 1] * y1


def _moe_combine(ys, pos, route):
    n = route.shape[0]
    d = V7X_SUBLANES * V7X_LANES
    tc = min(n, MOE_COMBINE_TOKENS)
    grid_spec = pltpu.PrefetchScalarGridSpec(
        num_scalar_prefetch=1, grid=(n // tc,),
        in_specs=[pl.BlockSpec(memory_space=pl.ANY),
                  pl.BlockSpec((tc, ROUTER_LANES), lambda i, p: (i, 0))],
        out_specs=pl.BlockSpec((tc, d), lambda i, p: (i, 0)),
        scratch_shapes=[pltpu.VMEM((2, 2, tc, V7X_SUBLANES, V7X_LANES), F32), pltpu.SemaphoreType.DMA((2,))])
    return pl.pallas_call(
        functools.partial(_moe_combine_kernel, tc, n, d),
        out_shape=jax.ShapeDtypeStruct((n, d), F32),
        grid_spec=grid_spec, compiler_params=_params(("arbitrary",)), name="moe_combine")(pos, ys, route)


def _moe_routed(h2t, route, counts, wg, wu, wd, layer):
    n = h2t.shape[0]
    assert h2t.shape[1:] == (V7X_SUBLANES, V7X_LANES)
    tm = MOE_ROW_TILE if n >= 32 * MOE_ROW_TILE else MOE_ROW_TILE_SMALL
    pos, tile, expert, lo, hi, first = _moe_tables(route, counts, tm)
    xs = _moe_scatter(h2t, pos)
    ys = _moe_experts(xs, tile, expert, lo, hi, first, wg, wu, wd, layer, tm)
    return _moe_combine(ys, pos, route)


def _final_kernel(bb, tm, d, xa_ref, moe_ref, gtp_ref, g_ref, o_ref):
    rows = bb * tm
    x = xa_ref[...].reshape(rows, d) + _per_seq(gtp_ref, bb, tm, d) * moe_ref[...].reshape(rows, d)
    o_ref[...] = _rmsnorm(x, g_ref[...]).reshape(bb, tm, d)


def _final_norm(xa, res, g):
    bsz, t, d = xa.shape
    bb, tm = _row_tiling(bsz, t)
    tile = pl.BlockSpec((bb, tm, d), lambda b, i: (b, i, 0))
    res_ins, res_specs = _moe_residual_operands(res, bb, tm)
    return pl.pallas_call(
        functools.partial(_final_kernel, bb, tm, d),
        out_shape=jax.ShapeDtypeStruct((bsz, t, d), F32),
        grid=(bsz // bb, t // tm),
        in_specs=[tile] + res_specs + [_full_spec(g)],
        out_specs=tile,
        compiler_params=_params(("arbitrary", "arbitrary")),
        name="final_norm",
    )(xa, *res_ins, g)


def _to_lanes(z, n_heads):
    bsz, t, _ = z.shape
    return z.reshape(bsz, t, n_heads, HEAD_DIM).transpose(1, 3, 0, 2).reshape(t, HEAD_DIM, bsz * n_heads)


def _from_lanes(z, bsz, n_heads):
    t = z.shape[0]
    return z.reshape(t, HEAD_DIM, bsz, n_heads).transpose(2, 0, 3, 1).reshape(bsz, t, n_heads * HEAD_DIM)


def _head_param(p, n_heads):
    return jnp.tile(p.reshape(n_heads, HEAD_DIM).T, (1, V7X_LANES // n_heads))


def _trunk(x, c, shift0, conv0, wkv0, W):
    bsz, t, d = x.shape
    n_heads = d // HEAD_DIM
    depth = W["w_in_bf"].shape[0]
    mod = _modulation(c, W["ada_w"], W["ada_b"])
    bufs, shifts, states = [], [], []
    res = None
    xa = x
    for l in range(depth):
        sh1, sc1, gt1, sh2, sc2, gt2 = (mod[l, :, i * d:(i + 1) * d][:, None, :] for i in range(6))
        row = lambda p: p[l][None, :]
        head_params = (W["rw_k_k"][l], W["rw_k_a"][l], W["rw_r_k"][l].reshape(-1), W["rw_lnx_g"][l], W["rw_lnx_b"][l])
        pre = _mixer_pre(
            xa, res, sh1, sc1, row(W["norm_mix_g"]), W["w_in_bf"], l, shift0[l][:, None, :], conv0[l],
            row(W["mu_shift"]), W["conv_w"][l], row(W["conv_b"]), row(W["conv_ln_g"]), row(W["conv_ln_b"]),
            row(W["rw_w0"]), W["lora_bf"][l], row(W["rw_a0"]), W["g_up_bf"][l],
            wkv_params=[p[None, :] for p in head_params] if wkv0 is None else None)
        if res is not None:
            xa, *pre = pre
        if wkv0 is None:
            yag, gb, y, convn, shiftn, s_l = pre
            states.append(s_l)
        else:
            yag, gb, r, lw, k, v, a, convn, shiftn = pre
            s0 = wkv0[l].transpose(2, 3, 0, 1).reshape(HEAD_DIM, HEAD_DIM, bsz * n_heads)
            y_l, s_l = _wkv_scan(*(_to_lanes(z, n_heads) for z in (r, lw, k, v, a)),
                                 *(_head_param(p, n_heads) for p in head_params), s0)
            y = _from_lanes(y_l, bsz, n_heads)
            states.append(s_l.reshape(HEAD_DIM, HEAD_DIM, bsz, n_heads).transpose(2, 3, 0, 1))
        x1, h2t, route, counts = _mixer_post(xa, yag, gb, y, gt1, sh2, sc2, row(W["norm_ffn_g"]), W["w_out_bf"][l],
                                             W["wr_hi"][l], W["wr_lo"][l], W["br"][l])
        moe = _moe_routed(h2t.reshape(bsz * t, V7X_SUBLANES, V7X_LANES), route.reshape(bsz * t, ROUTER_LANES),
                          counts[0], W["moe_w_gate"], W["moe_w_up"], W["moe_w_down"], l)
        xa, res = x1, (moe.reshape(bsz, t, d), gt2)
        bufs.append(convn)
        shifts.append(shiftn[:, 0, :])
    y_out = _final_norm(xa, res, W["final_norm_g"][None, :])
    return y_out, jnp.stack(bufs), jnp.stack(shifts), jnp.stack(states)


def kernel(x_prompt, x_sample, state_conv, state_shift, state_wkv, c_prompt, c_sample, norm_mix_g, norm_ffn_g, final_norm_g, ada_w, ada_b, w_in, mu_shift, conv_w, conv_b, conv_ln_g, conv_ln_b, rw_w0, rw_w_up, rw_a0, rw_a_up, rw_g_up, rw_k_k, rw_k_a, rw_r_k, rw_lnx_g, rw_lnx_b, w_out, moe_w_group, moe_b_group, moe_w_expert, moe_b_expert, moe_w_gate, moe_w_up, moe_w_down):
    depth, d, _ = w_in.shape
    zw = jnp.zeros_like(rw_w_up)
    lora = jnp.concatenate([jnp.concatenate([rw_w_up, zw], axis=2), jnp.concatenate([zw, rw_a_up], axis=2)], axis=1)
    pad = ROUTER_LANES - N_EXPERTS - N_GROUPS
    wr = jnp.concatenate([moe_w_expert, moe_w_group, jnp.zeros((depth, d, pad), F32)], axis=2)
    wr_hi = wr.astype(BF16)
    br = jnp.concatenate([moe_b_expert, moe_b_group, jnp.zeros((depth, pad), F32)], axis=1)[:, None, :]
    W = dict(norm_mix_g=norm_mix_g, norm_ffn_g=norm_ffn_g, final_norm_g=final_norm_g, ada_w=ada_w, ada_b=ada_b,
             w_in_bf=w_in.astype(BF16), mu_shift=mu_shift, conv_w=conv_w, conv_b=conv_b, conv_ln_g=conv_ln_g,
             conv_ln_b=conv_ln_b, rw_w0=rw_w0, lora_bf=lora.astype(BF16), rw_a0=rw_a0, g_up_bf=rw_g_up.astype(BF16),
             rw_k_k=rw_k_k, rw_k_a=rw_k_a, rw_r_k=rw_r_k, rw_lnx_g=rw_lnx_g, rw_lnx_b=rw_lnx_b,
             w_out_bf=w_out.astype(BF16), wr_hi=wr_hi, wr_lo=(wr - wr_hi.astype(F32)).astype(BF16), br=br,
             moe_w_gate=moe_w_gate, moe_w_up=moe_w_up, moe_w_down=moe_w_down)
    bp = x_prompt.shape[0]
    conv0 = jnp.zeros((depth, bp, CONV_BUF, d), x_prompt.dtype)
    shift0 = jnp.zeros((depth, bp, d), x_prompt.dtype)
    y_p, conv_p, shift_p, wkv_p = _trunk(x_prompt, c_prompt, shift0, conv0, None, W)
    y_s, conv_s, shift_s, wkv_s = _trunk(x_sample, c_sample, state_shift, state_conv, state_wkv, W)
    return (y_p, y_s, conv_p, shift_p, wkv_p.astype(state_wkv.dtype), conv_s, shift_s, wkv_s.astype(state_wkv.dtype))
```

```python
import functools

import jax
import jax.numpy as jnp
from jax import lax
from jax.experimental import pallas as pl
from jax.experimental.pallas import tpu as pltpu

F32 = jnp.float32
BF16 = jnp.bfloat16

HEAD_DIM = 64
CONV_WIDTH = 31
CONV_BUF = CONV_WIDTH - 1
LORA_W = 64
LORA_A = 64
N_GROUPS = 4
EXPERTS_PER_GROUP = 8
N_EXPERTS = N_GROUPS * EXPERTS_PER_GROUP
RMS_EPS = 1e-6
LN_EPS = 1e-5
GN_EPS = 64e-5

V7X_LANES = 128
V7X_SUBLANES = 8
V7X_VMEM_LIMIT_BYTES = 56 * 1024 * 1024

ROW_TILE = 256
SHORT_SEQ_ROWS = 128
HIST_ROWS = 32
HIST_PAD = HIST_ROWS - CONV_BUF
CONV_LANE_CHUNK = 256
WKV_CHUNK = 64
WKV_GROUP_HEADS = 4
WKV_GROUP = WKV_GROUP_HEADS * HEAD_DIM
WKV_INTERLEAVE_PER_MATMUL = 2
ROUTER_LANES = V7X_LANES
ROUTE_E1, ROUTE_E2, ROUTE_W1, ROUTE_W2, ROUTE_R1, ROUTE_R2 = 0, 1, 2, 3, 4, 5
MOE_ROW_TILE = 256
MOE_ROW_TILE_SMALL = 64
MOE_SCATTER_TOKENS = 256
MOE_COMBINE_TOKENS = 128


def _dot(a, b):
    return jnp.dot(a, b, preferred_element_type=F32)


def _sigmoid(x):
    return 1.0 / (1.0 + jnp.exp(-x))


def _rmsnorm(x, g):
    return x * lax.rsqrt(jnp.mean(x * x, axis=-1, keepdims=True) + RMS_EPS) * g


def _params(sem):
    return pltpu.CompilerParams(dimension_semantics=sem, vmem_limit_bytes=V7X_VMEM_LIMIT_BYTES)


def _row_tiling(bsz, t):
    tm = min(t, ROW_TILE)
    bb = max(1, min(bsz, SHORT_SEQ_ROWS // tm))
    assert t % tm == 0 and bsz % bb == 0 and tm % V7X_SUBLANES == 0
    assert bb == 1 or tm == t
    return bb, tm


def _per_seq(ref, bb, n, d):
    p = ref[...]
    if bb == 1:
        return p[0]
    return jnp.broadcast_to(p, (bb, n, d)).reshape(bb * n, d)


def _full_spec(a):
    return pl.BlockSpec(a.shape, lambda b, i: (0,) * a.ndim)


def _ada_kernel(c_ref, w_ref, b_ref, o_ref):
    c = c_ref[...]
    s = c * _sigmoid(c)
    o_ref[0] = _dot(s.astype(BF16), w_ref[0].astype(BF16)) + b_ref[0]


def _modulation(c, ada_w, ada_b):
    depth, d, n = ada_w.shape
    bsz = c.shape[0]
    tn = 1536
    return pl.pallas_call(
        _ada_kernel,
        out_shape=jax.ShapeDtypeStruct((depth, bsz, n), F32),
        grid=(depth, n // tn),
        in_specs=[
            pl.BlockSpec((bsz, d), lambda l, j: (0, 0)),
            pl.BlockSpec((1, d, tn), lambda l, j: (l, 0, j)),
            pl.BlockSpec((1, 1, tn), lambda l, j: (l, 0, j)),
        ],
        out_specs=pl.BlockSpec((1, bsz, tn), lambda l, j: (l, 0, j)),
        compiler_params=_params(("arbitrary", "arbitrary")),
        name="ada_modulation",
    )(c, ada_w, ada_b.reshape(depth, 1, n))


def _mixer_pre_kernel(fused, bb, tm, d, x_ref, sh_ref, sc_ref, g_ref, win_ref, shift0_ref, conv0_ref, mu_ref, cw_ref,
                      cb_ref, clg_ref, clb_ref, w0_ref, lora_ref, a0_ref, gup_ref, *refs):
    if fused:
        wkv_params, (ones_ref, tri_ref), post_in = refs[:5], refs[5:7], refs[7:16]
        (x1_ref, h2_ref, route_ref, counts_ref, convn_ref, shiftn_ref, sout_ref,
         ubuf_ref, z_ref, carry_ref, shifted_ref, s_ref, cnt_ref) = refs[16:]
    else:
        (yag_ref, gb_ref, r_ref, lw_ref, k_ref, v_ref, a_ref, convn_ref, shiftn_ref,
         ubuf_ref, z_ref, carry_ref, shifted_ref) = refs
    c_plain = 4 * d
    rows = bb * tm
    t = pl.program_id(1)
    nt = pl.num_programs(1)

    @pl.when(t == 0)
    def _():
        hp = _per_seq(shift0_ref, bb, V7X_SUBLANES, d)
        hp = jnp.broadcast_to(hp, (bb * V7X_SUBLANES, d)).astype(BF16)
        carry_ref[...] = _dot(hp, win_ref[:, c_plain:])
        ubuf_ref[:, pl.ds(0, HIST_PAD), :] = jnp.zeros((bb, HIST_PAD, d), F32)
        ubuf_ref[:, pl.ds(HIST_PAD, CONV_BUF), :] = conv0_ref[...]
        if fused:
            s_ref[...] = jnp.zeros_like(s_ref)

    if fused:
        @pl.when((pl.program_id(0) == 0) & (t == 0))
        def _():
            cnt_ref[...] = jnp.zeros_like(cnt_ref)

    x = x_ref[...].reshape(rows, d)
    h = _rmsnorm(x, g_ref[...]) * (1.0 + _per_seq(sc_ref, bb, tm, d)) + _per_seq(sh_ref, bb, tm, d)
    shiftn_ref[...] = h.reshape(bb, tm, d)[:, tm - 1:tm, :]
    hb = h.astype(BF16)
    cur = _dot(hb, win_ref[:, c_plain:])
    proj = _dot(hb, win_ref[:, :c_plain])

    prev = pltpu.roll(cur, 1, 0)
    row = lax.broadcasted_iota(jnp.int32, cur.shape, 0)
    if bb == 1:
        prev = jnp.where(row == 0, carry_ref[pl.ds(0, 1), :], prev)
        carry_ref[pl.ds(0, 1), :] = cur[tm - 1:tm, :]
    else:
        prev = jnp.where((row & (tm - 1)) == 0, carry_ref[...], prev)
    rw = cur + (prev - cur) * mu_ref[...]

    r = rw[:, 0:d]
    k = rw[:, d:2 * d]
    v = rw[:, 2 * d:3 * d]
    xwa = rw[:, 3 * d:3 * d + LORA_W + LORA_A]
    xg = rw[:, 3 * d + LORA_W + LORA_A:]
    lane = lax.broadcasted_iota(jnp.int32, xwa.shape, 1)
    lhs = jnp.where(lane < LORA_W, jnp.tanh(xwa), xwa)
    za = _dot(lhs.astype(BF16), lora_ref[...])
    log_decay = -jnp.exp(-0.5) * _sigmoid(w0_ref[...] + za[:, 0:d])
    a = _sigmoid(a0_ref[...] + za[:, d:])
    gate = _dot(_sigmoid(xg).astype(BF16), gup_ref[...])
    blk = (bb, tm, d)

    u = proj[:, 0:d] * _sigmoid(proj[:, d:2 * d])
    ubuf_ref[:, pl.ds(HIST_ROWS, tm), :] = u.reshape(blk)
    rb = min(tm, 32)
    lc = CONV_LANE_CHUNK
    span = tm + HIST_ROWS - V7X_SUBLANES

    def conv_shift(b, c0):
        for s in range(1, V7X_SUBLANES):
            shifted_ref[s - 1] = ubuf_ref[b, pl.ds(s, span), c0:c0 + lc]

    def conv_rows(b, c0, r0):
        acc = jnp.broadcast_to(cb_ref[:, c0:c0 + lc], (rb, lc))
        for tap in range(CONV_WIDTH):
            s = (HIST_PAD + tap) % V7X_SUBLANES
            base = r0 + HIST_PAD + tap - s
            if s == 0:
                src = ubuf_ref[b, pl.ds(base, rb), c0:c0 + lc]
            else:
                src = shifted_ref[s - 1, pl.ds(base, rb), :]
            acc = acc + cw_ref[pl.ds(tap, 1), c0:c0 + lc] * src
        z_ref[pl.ds(b * tm + r0, rb), c0:c0 + lc] = acc

    conv_work = []
    for c0 in range(0, d, lc):
        for b in range(bb):
            conv_work.append(functools.partial(conv_shift, b, c0))
            conv_work += [functools.partial(conv_rows, b, c0, r0) for r0 in range(0, tm, rb)]

    if fused:
        y = _wkv_chunk_math(tm, d, r, log_decay, k, v, a, wkv_params, ones_ref, tri_ref, s_ref, conv_work)
    else:
        r_ref[...] = r.reshape(blk)
        k_ref[...] = k.reshape(blk)
        v_ref[...] = v.reshape(blk)
        lw_ref[...] = log_decay.reshape(blk)
        a_ref[...] = a.reshape(blk)
    gb = _sigmoid(proj[:, 3 * d:4 * d]) * gate

    for work in conv_work:
        work()
    z = z_ref[...]
    mu = jnp.mean(z, axis=-1, keepdims=True)
    zc = z - mu
    var = jnp.mean(zc * zc, axis=-1, keepdims=True)
    zn = zc * lax.rsqrt(var + LN_EPS) * clg_ref[...] + clb_ref[...]
    yag = _sigmoid(proj[:, 2 * d:3 * d]) * (zn * _sigmoid(zn))

    convn_ref[...] = ubuf_ref[:, pl.ds(tm + HIST_PAD, CONV_BUF), :]
    if tm >= HIST_ROWS:
        ubuf_ref[:, pl.ds(0, HIST_ROWS), :] = ubuf_ref[:, pl.ds(tm, HIST_ROWS), :]

    if fused:
        _mixer_post_math(bb, tm, d, x, yag + gb * y, *post_in, cnt_ref, x1_ref, h2_ref, route_ref, counts_ref)
    else:
        yag_ref[...] = yag.reshape(blk)
        gb_ref[...] = gb.reshape(blk)

    if fused:
        @pl.when(t == nt - 1)
        def _():
            for hd in range(d // HEAD_DIM):
                g, hs = divmod(hd, WKV_GROUP_HEADS)
                hs = slice(hs * HEAD_DIM, (hs + 1) * HEAD_DIM)
                sout_ref[0, hd] = s_ref[g, hs, hs]


def _mixer_pre(x, sh, sc, g, win_bf, layer, shift0, conv0, mu, cw, cb, clg, clb, w0, lora_bf, a0, gup_bf,
               wkv_params=None, post=None):
    bsz, t, d = x.shape
    bb, tm = _row_tiling(bsz, t)
    assert t == tm or tm >= HIST_ROWS
    assert bb == 1 or tm == V7X_SUBLANES
    fused = wkv_params is not None
    c_in = win_bf.shape[2]
    c_rwkv = c_in - 4 * d
    tile = pl.BlockSpec((bb, tm, d), lambda b, i: (b, i, 0))
    perb = pl.BlockSpec((bb, 1, d), lambda b, i: (b, 0, 0))
    hist = pl.BlockSpec((bb, CONV_BUF, d), lambda b, i: (b, 0, 0))
    ins = [x, sh, sc, g, win_bf, shift0, conv0, mu, cw, cb, clg, clb, w0, lora_bf, a0, gup_bf]
    specs = [tile, perb, perb, _full_spec(g),
             pl.BlockSpec((None, d, c_in), lambda b, i: (layer, 0, 0), pipeline_mode=pl.Buffered(1)),
             perb, hist] + [_full_spec(p) for p in (mu, cw, cb, clg, clb, w0, lora_bf, a0, gup_bf)]
    act = jax.ShapeDtypeStruct((bsz, t, d), F32)
    state_shapes = [jax.ShapeDtypeStruct((bsz, CONV_BUF, d), F32), jax.ShapeDtypeStruct((bsz, 1, d), F32)]
    scratch = [pltpu.VMEM((bb, tm + HIST_ROWS, d), F32), pltpu.VMEM((bb * tm, d), F32),
               pltpu.VMEM((bb * V7X_SUBLANES, c_rwkv), F32),
               pltpu.VMEM((V7X_SUBLANES - 1, tm + HIST_ROWS - V7X_SUBLANES, CONV_LANE_CHUNK), F32)]
    if fused:
        assert bb == 1 and tm % WKV_CHUNK == 0 and d % WKV_GROUP == 0
        n_heads = d // HEAD_DIM
        consts = _wkv_constants()
        post_ins, post_specs, post_shapes, post_out_specs, post_scratch = _mixer_post_operands(bsz, t, d, bb, tm, *post)
        ins += list(wkv_params) + list(consts) + post_ins
        specs += [_full_spec(p) for p in wkv_params] + [_full_spec(c) for c in consts] + post_specs
        out_shape = post_shapes + state_shapes + [jax.ShapeDtypeStruct((bsz, n_heads, HEAD_DIM, HEAD_DIM), F32)]
        out_specs = post_out_specs + [hist, perb,
                                      pl.BlockSpec((1, n_heads, HEAD_DIM, HEAD_DIM), lambda b, i: (b, 0, 0, 0))]
        scratch += [pltpu.VMEM((d // WKV_GROUP, WKV_GROUP, WKV_GROUP), F32)] + post_scratch
    else:
        out_shape = [act] * 7 + state_shapes
        out_specs = [tile] * 7 + [hist, perb]
    return pl.pallas_call(
        functools.partial(_mixer_pre_kernel, fused, bb, tm, d),
        out_shape=out_shape,
        grid=(bsz // bb, t // tm),
        in_specs=specs,
        out_specs=out_specs,
        scratch_shapes=scratch,
        compiler_params=_params(("arbitrary", "arbitrary")),
        name="mixer_pre",
    )(*ins)


def _wkv_kernel(tc, r_ref, lw_ref, k_ref, v_ref, a_ref, kkp_ref, kap_ref, rkp_ref, lng_ref, lnb_ref, s0_ref,
                y_ref, s_ref, o_ref):
    n = HEAD_DIM

    @pl.when(pl.program_id(1) == 0)
    def _():
        s_ref[...] = s0_ref[...]

    def step(tt, carry):
        r_t = r_ref[tt]
        k_t = k_ref[tt]
        a_t = a_ref[tt]
        w_t = jnp.exp(lw_ref[tt])
        kk = k_t * kkp_ref[...]
        kk = kk * lax.rsqrt(jnp.sum(kk * kk, axis=0, keepdims=True) + 1e-12)
        b_t = kk * a_t
        k2 = k_t * (1.0 + (a_t - 1.0) * kap_ref[...])

        def row(i, c):
            si = s_ref[i]
            sa = jnp.sum(si * kk, axis=0, keepdims=True)
            vi = v_ref[tt, pl.ds(i, 1), :]
            sn = si * w_t - sa * b_t + vi * k2
            s_ref[i] = sn
            o_ref[pl.ds(i, 1), :] = jnp.sum(sn * r_t, axis=0, keepdims=True)
            return c

        lax.fori_loop(0, n, row, 0, unroll=8)
        o = o_ref[...]
        mu = jnp.mean(o, axis=0, keepdims=True)
        oc = o - mu
        var = jnp.mean(oc * oc, axis=0, keepdims=True)
        yn = oc * lax.rsqrt(var + GN_EPS) * lng_ref[...] + lnb_ref[...]
        bonus = jnp.sum(r_t * k2 * rkp_ref[...], axis=0, keepdims=True) * v_ref[tt]
        y_ref[tt] = yn + bonus
        return carry

    lax.fori_loop(0, tc, step, 0)


def _wkv_scan(r, w, k, v, a, kkp, kap, rkp, lng, lnb, s0):
    t, n, nl = r.shape
    tc = min(t, 32)
    lb = V7X_LANES
    seq = pl.BlockSpec((tc, n, lb), lambda l, i: (i, 0, l))
    par = pl.BlockSpec((n, lb), lambda l, i: (0, 0))
    st = pl.BlockSpec((n, n, lb), lambda l, i: (0, 0, l))
    return pl.pallas_call(
        functools.partial(_wkv_kernel, tc),
        out_shape=[jax.ShapeDtypeStruct((t, n, nl), F32), jax.ShapeDtypeStruct((n, n, nl), F32)],
        grid=(nl // lb, t // tc),
        in_specs=[seq] * 5 + [par] * 5 + [st],
        out_specs=[seq, st],
        scratch_shapes=[pltpu.VMEM((n, lb), F32)],
        compiler_params=_params(("arbitrary", "arbitrary")),
        name="wkv_scan",
    )(r, w, k, v, a, kkp, kap, rkp, lng, lnb, s0)


def _bmm(a, b):
    return jnp.einsum("gck,gkn->gcn", a, b, preferred_element_type=F32)


def _bmm_nt(a, b):
    return jnp.einsum("gck,gnk->gcn", a, b, preferred_element_type=F32)


def _bmm_tn(a, b):
    return jnp.einsum("gtc,gtn->gcn", a, b, preferred_element_type=F32)


def _split_bf16(x):
    hi = x.astype(BF16)
    return hi, (x - hi.astype(F32)).astype(BF16)


def _wkv_constants():
    li = lax.broadcasted_iota(jnp.int32, (WKV_GROUP, WKV_GROUP), 0) // HEAD_DIM
    lj = lax.broadcasted_iota(jnp.int32, (WKV_GROUP, WKV_GROUP), 1) // HEAD_DIM
    ti = lax.broadcasted_iota(jnp.int32, (WKV_CHUNK, WKV_CHUNK), 0)
    tj = lax.broadcasted_iota(jnp.int32, (WKV_CHUNK, WKV_CHUNK), 1)
    return (li == lj).astype(BF16), (ti >= tj).astype(BF16)


def _wkv_chunk_math(tt, d, r, lw, k, v, a, params, ones_ref, tri_ref, s_ref, interleaved):
    c = WKV_CHUNK
    gw = WKV_GROUP
    ng = d // gw
    nc = tt // c
    n_items = nc * ng
    kkp_ref, kap_ref, rkp_ref, lng_ref, lnb_ref = params

    lane = lax.broadcasted_iota(jnp.int32, (1, c, gw), 2)
    rowi = lax.broadcasted_iota(jnp.int32, (1, c, gw), 1)
    head_of_lane = lax.shift_right_logical(lane, 6)
    hmask = [head_of_lane == h for h in range(WKV_GROUP_HEADS)]
    tcol = lane & (c - 1)
    strict = tcol < rowi
    incl = tcol <= rowi
    bl = lax.broadcasted_iota(jnp.int32, (1, gw, gw), 2)
    br = lax.broadcasted_iota(jnp.int32, (1, gw, gw), 1)
    blockmask = lax.shift_right_logical(bl, 6) == lax.shift_right_logical(br, 6)

    def items(x):
        x = x.reshape(nc, c, d)
        return jnp.stack([x[:, :, g * gw:(g + 1) * gw] for g in range(ng)], axis=1).reshape(n_items, c, gw)

    def per_group(ref):
        p = jnp.stack([ref[:, g * gw:(g + 1) * gw] for g in range(ng)], axis=0)
        return jnp.broadcast_to(p[None], (nc, ng, 1, gw)).reshape(n_items, 1, gw)

    def bd(xf):
        return jnp.concatenate([jnp.where(hmask[h], xf, 0.0) for h in range(WKV_GROUP_HEADS)], axis=1).astype(BF16)

    def segsum(x):
        shp = x.shape
        hi, lo = _split_bf16(x.reshape(-1, gw))
        return (_dot(hi, ones_ref[...]) + _dot(lo, ones_ref[...])).reshape(shp)

    def filled(y):
        for _ in range(WKV_INTERLEAVE_PER_MATMUL):
            if interleaved:
                interleaved.pop(0)()
        return y

    def bmm(x, y):
        return filled(_bmm(x, y))

    def bmm_nt(x, y):
        return filled(_bmm_nt(x, y))

    def bmm_tn(x, y):
        return filled(_bmm_tn(x, y))

    r, lw, k, v, a = (items(x) for x in (r, lw, k, v, a))
    kk = k * per_group(kkp_ref)
    kk = kk * lax.rsqrt(segsum(kk * kk) + 1e-12)
    b = kk * a
    k2 = k * (1.0 + (a - 1.0) * per_group(kap_ref))
    lw_hi, lw_lo = _split_bf16(lw)
    tri = jnp.broadcast_to(tri_ref[...][None], (n_items, c, c))
    cum = bmm(tri, lw_hi) + bmm(tri, lw_lo)
    cl = cum[:, c - 1:c, :]
    at = -(kk * jnp.exp(cum - lw))
    rt = r * jnp.exp(cum)
    winv = jnp.exp(-cum)
    wrem = jnp.exp(cl - cum)
    wc = jnp.exp(cl)
    lhs = jnp.concatenate([at, rt], axis=1).astype(BF16)
    rhs = jnp.concatenate([bd(b * winv), bd(k2 * winv)], axis=1)
    sc = bmm_nt(lhs, rhs)
    l_ab = jnp.where(strict, sc[:, 0:c, 0:gw], 0.0)
    l_ak = jnp.where(strict, sc[:, 0:c, gw:], 0.0)
    l_rb = jnp.where(incl, sc[:, c:, 0:gw], 0.0)
    l_rk = jnp.where(incl, sc[:, c:, gw:], 0.0)
    pk = l_ab
    tm = l_ab
    pbd = bd(pk)
    for _ in range(c.bit_length() - 2):
        pk = bmm(pk.astype(BF16), pbd)
        pbd = bd(pk)
        tm = tm + pk + bmm(tm.astype(BF16), pbd)
    tmb = tm.astype(BF16)
    vbd = bd(v)
    ah = at + bmm(tmb, bd(at))
    x1 = bmm(l_ak.astype(BF16), vbd)
    gm = x1 + bmm(tmb, bd(x1))
    l_rbb = l_rb.astype(BF16)
    rh = (rt + bmm(l_rbb, bd(ah))).astype(BF16)
    o_ind = bmm(jnp.concatenate([l_rbb, l_rk.astype(BF16)], axis=2), jnp.concatenate([bd(gm), vbd], axis=1))
    bb = (b * wrem).astype(BF16)
    kb = (k2 * wrem).astype(BF16)
    m = jnp.where(blockmask, bmm_tn(ah.astype(BF16), bb), 0.0).astype(BF16)
    n = jnp.where(blockmask, bmm_tn(jnp.concatenate([gm, v], axis=1).astype(BF16),
                                    jnp.concatenate([bb, kb], axis=1)), 0.0)

    s = s_ref[...]
    outs = []
    for ci in range(nc):
        sl = slice(ci * ng, (ci + 1) * ng)
        sb = s.astype(BF16)
        outs.append(bmm_nt(rh[sl], sb) + o_ind[sl])
        s = s * wc[sl] + bmm(sb, m[sl]) + n[sl]
    s_ref[...] = s

    inv_n = 1.0 / HEAD_DIM
    o = jnp.stack(outs, axis=0).reshape(n_items, c, gw)
    mu = segsum(o) * inv_n
    oc = o - mu
    var = segsum(oc * oc) * inv_n
    yn = oc * lax.rsqrt(var + GN_EPS) * per_group(lng_ref) + per_group(lnb_ref)
    bonus = segsum(r * k2 * per_group(rkp_ref)) * v
    y = (yn + bonus).reshape(nc, ng, c, gw)
    return jnp.concatenate([y[:, g].reshape(tt, gw) for g in range(ng)], axis=1)


def _mixer_post_kernel(bb, tm, d, x_ref, yag_ref, gb_ref, y_ref, *refs):
    *post_in, x1_ref, h2_ref, route_ref, counts_ref, cnt_ref = refs

    @pl.when((pl.program_id(0) == 0) & (pl.program_id(1) == 0))
    def _():
        cnt_ref[...] = jnp.zeros_like(cnt_ref)

    rows = bb * tm
    merged = (yag_ref[...] + gb_ref[...] * y_ref[...]).reshape(rows, d)
    _mixer_post_math(bb, tm, d, x_ref[...].reshape(rows, d), merged, *post_in, cnt_ref,
                     x1_ref, h2_ref, route_ref, counts_ref)


def _mixer_post_math(bb, tm, d, x, merged, gt1_ref, sh_ref, sc_ref, g_ref, wout_ref, wrh_ref, wrl_ref, br_ref,
                     tri_ref, cnt_ref, x1_ref, h2_ref, route_ref, counts_ref):
    x1 = x + _per_seq(gt1_ref, bb, tm, d) * _dot(merged.astype(BF16), wout_ref[...])
    x1_ref[...] = x1.reshape(bb, tm, d)
    h2 = _rmsnorm(x1, g_ref[...]) * (1.0 + _per_seq(sc_ref, bb, tm, d)) + _per_seq(sh_ref, bb, tm, d)
    h2_ref[...] = h2.reshape(bb, tm, V7X_SUBLANES, d // V7X_SUBLANES)
    h_hi = h2.astype(BF16)
    h_lo = (h2 - h_hi.astype(F32)).astype(BF16)
    logits = _dot(h_hi, wrh_ref[...]) + (_dot(h_lo, wrh_ref[...]) + _dot(h_hi, wrl_ref[...])) + br_ref[...]

    lane_i = lax.broadcasted_iota(jnp.int32, logits.shape, 1)
    lane = lane_i.astype(F32)
    grp_of_lane = lax.shift_right_logical(lane_i, 3).astype(F32)
    neg = -jnp.inf
    none = 1e9
    is_g = (lane_i >= N_EXPERTS) & (lane_i < N_EXPERTS + N_GROUPS)
    glv = jnp.where(is_g, logits, neg)
    gmax = jnp.max(glv, axis=-1, keepdims=True)
    gidx = jnp.min(jnp.where(is_g & (glv == gmax), lane - N_EXPERTS, none), axis=-1, keepdims=True)
    pg = 1.0 / jnp.sum(jnp.where(is_g, jnp.exp(glv - gmax), 0.0), axis=-1, keepdims=True)
    in_grp = (lane_i < N_EXPERTS) & (grp_of_lane == gidx)
    ev = jnp.where(in_grp, logits, neg)
    v1 = jnp.max(ev, axis=-1, keepdims=True)
    i1 = jnp.min(jnp.where(in_grp & (ev == v1), lane, none), axis=-1, keepdims=True)
    ev2 = jnp.where(lane == i1, neg, ev)
    v2 = jnp.max(ev2, axis=-1, keepdims=True)
    i2 = jnp.min(jnp.where(in_grp & (ev2 == v2) & (lane != i1), lane, none), axis=-1, keepdims=True)
    e2 = jnp.exp(v2 - v1)
    den = 1.0 + e2
    hit = ((lane == i1) | (lane == i2)).astype(F32)
    before = cnt_ref[pl.ds(0, 1), :] + _dot(tri_ref[...], hit.astype(BF16))
    r1 = jnp.sum(jnp.where(lane == i1, before, 0.0), axis=-1, keepdims=True)
    r2 = jnp.sum(jnp.where(lane == i2, before, 0.0), axis=-1, keepdims=True)
    counts = cnt_ref[pl.ds(0, 1), :] + jnp.sum(hit, axis=0, keepdims=True)
    cnt_ref[pl.ds(0, 1), :] = counts
    counts_ref[...] = jnp.broadcast_to(counts, counts_ref.shape)
    route = (jnp.where(lane_i == ROUTE_E1, i1, 0.0) + jnp.where(lane_i == ROUTE_E2, i2, 0.0)
             + jnp.where(lane_i == ROUTE_W1, (1.0 / den) * pg, 0.0) + jnp.where(lane_i == ROUTE_W2, (e2 / den) * pg, 0.0)
             + jnp.where(lane_i == ROUTE_R1, r1, 0.0) + jnp.where(lane_i == ROUTE_R2, r2, 0.0))
    route_ref[...] = route.reshape(bb, tm, ROUTER_LANES)


def _mixer_post_operands(bsz, t, d, bb, tm, gt1, sh, sc, g, wout_bf, layer, wr_hi, wr_lo, br):
    rows = bb * tm
    tile = pl.BlockSpec((bb, tm, d), lambda b, i: (b, i, 0))
    perb = pl.BlockSpec((bb, 1, d), lambda b, i: (b, 0, 0))
    ri = lax.broadcasted_iota(jnp.int32, (rows, rows), 0)
    ci = lax.broadcasted_iota(jnp.int32, (rows, rows), 1)
    tri = (ci < ri).astype(BF16)
    ins = [gt1, sh, sc, g, wout_bf, wr_hi, wr_lo, br, tri]
    specs = [perb, perb, perb, _full_spec(g), pl.BlockSpec((None, d, d), lambda b, i: (layer, 0, 0))]
    specs += [_full_spec(p) for p in (wr_hi, wr_lo, br, tri)]
    token_tiles = (bb, tm, V7X_SUBLANES, d // V7X_SUBLANES)
    out_shapes = [jax.ShapeDtypeStruct((bsz, t, d), F32),
                  jax.ShapeDtypeStruct((bsz, t, V7X_SUBLANES, d // V7X_SUBLANES), F32),
                  jax.ShapeDtypeStruct((bsz, t, ROUTER_LANES), F32),
                  jax.ShapeDtypeStruct((V7X_SUBLANES, ROUTER_LANES), F32)]
    out_specs = [tile, pl.BlockSpec(token_tiles, lambda b, i: (b, i, 0, 0)),
                 pl.BlockSpec((bb, tm, ROUTER_LANES), lambda b, i: (b, i, 0)),
                 pl.BlockSpec((V7X_SUBLANES, ROUTER_LANES), lambda b, i: (0, 0))]
    return ins, specs, out_shapes, out_specs, [pltpu.VMEM((V7X_SUBLANES, ROUTER_LANES), F32)]


def _mixer_post(x, yag, gb, y, *post):
    bsz, t, d = x.shape
    bb, tm = _row_tiling(bsz, t)
    tile = pl.BlockSpec((bb, tm, d), lambda b, i: (b, i, 0))
    ins, specs, out_shapes, out_specs, scratch = _mixer_post_operands(bsz, t, d, bb, tm, *post)
    return pl.pallas_call(
        functools.partial(_mixer_post_kernel, bb, tm, d),
        out_shape=out_shapes,
        grid=(bsz // bb, t // tm),
        in_specs=[tile] * 4 + specs,
        out_specs=out_specs,
        scratch_shapes=scratch,
        compiler_params=_params(("arbitrary", "arbitrary")),
        name="mixer_post",
    )(x, yag, gb, y, *ins)


def _moe_tables(route, counts, tm):
    n = route.shape[0]
    nt = 2 * n // tm
    cnt = counts[:N_EXPERTS].astype(jnp.int32)
    ends = jnp.cumsum(cnt)
    starts = ends - cnt
    ids = jnp.arange(N_EXPERTS, dtype=jnp.int32)[None, :]

    def position(e_lane, r_lane):
        e = route[:, e_lane].astype(jnp.int32)
        return jnp.sum(jnp.where(e[:, None] == ids, starts[None, :], 0), axis=1) + route[:, r_lane].astype(jnp.int32)

    pos = jnp.concatenate([position(ROUTE_E1, ROUTE_R1), position(ROUTE_E2, ROUTE_R2)])
    bounds = jnp.sort(jnp.concatenate([jnp.arange(nt + 1, dtype=jnp.int32) * tm, ends]))
    lo_abs, hi_abs = bounds[:-1], bounds[1:]
    tile = jnp.minimum(lo_abs // tm, nt - 1)
    nonzero = hi_abs > lo_abs
    expert = jnp.minimum(jnp.sum((ends[None, :] <= lo_abs[:, None]).astype(jnp.int32), axis=1), N_EXPERTS - 1)
    expert = jnp.maximum(lax.cummax(jnp.where(nonzero, expert, -1)), 0)
    prev_tile = jnp.concatenate([jnp.full((1,), -1, jnp.int32), lax.cummax(jnp.where(nonzero, tile, -1))[:-1]])
    first = (nonzero & (tile > prev_tile)).astype(jnp.int32)
    return pos, tile, expert, lo_abs - tile * tm, hi_abs - tile * tm, first


def _moe_scatter_kernel(ts, n, pos_ref, h_ref, xs_hbm, sem):
    i = pl.program_id(0)

    def copies(start):
        def body(j, c):
            for s in range(2):
                cp = pltpu.make_async_copy(h_ref.at[j], xs_hbm.at[pos_ref[s * n + i * ts + j]], sem)
                if start:
                    cp.start(priority=s)
                else:
                    cp.wait()
            return c
        lax.fori_loop(0, ts, body, 0, unroll=8)

    copies(True)
    copies(False)


def _moe_scatter(h2t, pos):
    n = h2t.shape[0]
    ts = min(n, MOE_SCATTER_TOKENS)
    grid_spec = pltpu.PrefetchScalarGridSpec(
        num_scalar_prefetch=1, grid=(n // ts,),
        in_specs=[pl.BlockSpec((ts, V7X_SUBLANES, V7X_LANES), lambda i, p: (i, 0, 0))],
        out_specs=pl.BlockSpec(memory_space=pl.ANY),
        scratch_shapes=[pltpu.SemaphoreType.DMA(())])
    return pl.pallas_call(
        functools.partial(_moe_scatter_kernel, ts, n),
        out_shape=jax.ShapeDtypeStruct((2 * n, V7X_SUBLANES, V7X_LANES), F32),
        grid_spec=grid_spec, compiler_params=_params(("arbitrary",)), name="moe_scatter")(pos, h2t)


def _moe_experts_kernel(tm, d, tile_ref, exp_ref, lo_ref, hi_ref, first_ref, xs_ref, wg_ref, wu_ref, wd_ref, ys_ref):
    w = pl.program_id(0)
    lo = lo_ref[w]
    hi = hi_ref[w]

    @pl.when(hi > lo)
    def _():
        x = xs_ref[...].reshape(tm, d).astype(BF16)
        hg = _dot(x, wg_ref[0].astype(BF16))
        hu = _dot(x, wu_ref[0].astype(BF16))
        act = hg * _sigmoid(hg) * hu
        y = _dot(act.astype(BF16), wd_ref[0].astype(BF16))
        row = lax.broadcasted_iota(jnp.int32, (tm, 1), 0)
        y = jnp.where((row >= lo) & (row < hi), y, 0.0).reshape(tm, V7X_SUBLANES, V7X_LANES)

        @pl.when(first_ref[w] == 1)
        def _():
            ys_ref[...] = y

        @pl.when(first_ref[w] == 0)
        def _():
            ys_ref[...] += y


def _moe_experts(xs, tile, expert, lo, hi, first, wg, wu, wd, layer, tm):
    d = V7X_SUBLANES * V7X_LANES
    f = wg.shape[-1]
    row_tile = pl.BlockSpec((tm, V7X_SUBLANES, V7X_LANES), lambda w, t, e, *_: (t[w], 0, 0))
    grid_spec = pltpu.PrefetchScalarGridSpec(
        num_scalar_prefetch=5, grid=(tile.shape[0],),
        in_specs=[row_tile,
                  pl.BlockSpec((None, 1, d, f), lambda w, t, e, *_: (layer, e[w], 0, 0)),
                  pl.BlockSpec((None, 1, d, f), lambda w, t, e, *_: (layer, e[w], 0, 0)),
                  pl.BlockSpec((None, 1, f, d), lambda w, t, e, *_: (layer, e[w], 0, 0))],
        out_specs=row_tile)
    return pl.pallas_call(
        functools.partial(_moe_experts_kernel, tm, d),
        out_shape=jax.ShapeDtypeStruct(xs.shape, F32),
        grid_spec=grid_spec, compiler_params=_params(("arbitrary",)), name="moe_experts")(
            tile, expert, lo, hi, first, xs, wg, wu, wd)


def _moe_combine_kernel(final, bb, tm, d, n, pos_ref, ys_hbm, route_ref, x_ref, gt_ref, *refs):
    if final:
        g_ref, o_ref, gbuf, sem = refs
    else:
        o_ref, gbuf, sem = refs
    rows = bb * tm
    blk = pl.program_id(0) * pl.num_programs(1) + pl.program_id(1)
    n_blk = pl.num_programs(0) * pl.num_programs(1)
    slot = lax.rem(blk, 2)

    def copies(k, sl, start):
        def body(j, c):
            for s in range(2):
                cp = pltpu.make_async_copy(ys_hbm.at[pos_ref[s * n + k * rows + j]], gbuf.at[sl, s, j], sem.at[sl])
                if start:
                    cp.start(priority=s)
                else:
                    cp.wait()
            return c
        lax.fori_loop(0, rows, body, 0, unroll=8)

    @pl.when(blk == 0)
    def _():
        copies(0, 0, True)

    @pl.when(blk + 1 < n_blk)
    def _():
        copies(blk + 1, 1 - slot, True)

    copies(blk, slot, False)
    route = route_ref[...].reshape(rows, ROUTER_LANES)
    y0 = jnp.concatenate([gbuf[slot, 0, :, s, :] for s in range(V7X_SUBLANES)], axis=1)
    y1 = jnp.concatenate([gbuf[slot, 1, :, s, :] for s in range(V7X_SUBLANES)], axis=1)
    moe = route[:, ROUTE_W1:ROUTE_W1 + 1] * y0 + route[:, ROUTE_W2:ROUTE_W2 + 1] * y1
    x = x_ref[...].reshape(rows, d) + _per_seq(gt_ref, bb, tm, d) * moe
    if final:
        x = _rmsnorm(x, g_ref[...])
    o_ref[...] = x.reshape(bb, tm, d)


def _moe_combine(ys, pos, route, x1, gate, final_g):
    bsz, t, d = x1.shape
    bb, tm = _row_tiling(bsz, t)
    rows = bb * tm
    tile = pl.BlockSpec((bb, tm, d), lambda b, i, p: (b, i, 0))
    ins = [ys, route, x1, gate]
    in_specs = [pl.BlockSpec(memory_space=pl.ANY),
                pl.BlockSpec((bb, tm, ROUTER_LANES), lambda b, i, p: (b, i, 0)),
                tile, pl.BlockSpec((bb, 1, d), lambda b, i, p: (b, 0, 0))]
    if final_g is not None:
        ins.append(final_g)
        in_specs.append(pl.BlockSpec(final_g.shape, lambda b, i, p: (0, 0)))
    grid_spec = pltpu.PrefetchScalarGridSpec(
        num_scalar_prefetch=1, grid=(bsz // bb, t // tm), in_specs=in_specs, out_specs=tile,
        scratch_shapes=[pltpu.VMEM((2, 2, rows, V7X_SUBLANES, V7X_LANES), F32), pltpu.SemaphoreType.DMA((2,))])
    return pl.pallas_call(
        functools.partial(_moe_combine_kernel, final_g is not None, bb, tm, d, bsz * t),
        out_shape=jax.ShapeDtypeStruct((bsz, t, d), F32),
        grid_spec=grid_spec, compiler_params=_params(("arbitrary", "arbitrary")), name="moe_combine")(pos, *ins)


def _moe_routed(h2t, route, counts, wg, wu, wd, layer, x1, gate, final_g):
    bsz, t, d = x1.shape
    n = bsz * t
    assert h2t.shape[2:] == (V7X_SUBLANES, V7X_LANES)
    tm = MOE_ROW_TILE if n >= 32 * MOE_ROW_TILE else MOE_ROW_TILE_SMALL
    pos, tile, expert, lo, hi, first = _moe_tables(route.reshape(n, ROUTER_LANES), counts, tm)
    xs = _moe_scatter(h2t.reshape(n, V7X_SUBLANES, V7X_LANES), pos)
    ys = _moe_experts(xs, tile, expert, lo, hi, first, wg, wu, wd, layer, tm)
    return _moe_combine(ys, pos, route, x1, gate, final_g)


def _to_lanes(z, n_heads):
    bsz, t, _ = z.shape
    return z.reshape(bsz, t, n_heads, HEAD_DIM).transpose(1, 3, 0, 2).reshape(t, HEAD_DIM, bsz * n_heads)


def _from_lanes(z, bsz, n_heads):
    t = z.shape[0]
    return z.reshape(t, HEAD_DIM, bsz, n_heads).transpose(2, 0, 3, 1).reshape(bsz, t, n_heads * HEAD_DIM)


def _head_param(p, n_heads):
    return jnp.tile(p.reshape(n_heads, HEAD_DIM).T, (1, V7X_LANES // n_heads))


def _trunk(x, c, shift0, conv0, wkv0, W):
    bsz, t, d = x.shape
    n_heads = d // HEAD_DIM
    depth = W["w_in_bf"].shape[0]
    mod = _modulation(c, W["ada_w"], W["ada_b"])
    bufs, shifts, states = [], [], []
    for l in range(depth):
        sh1, sc1, gt1, sh2, sc2, gt2 = (mod[l, :, i * d:(i + 1) * d][:, None, :] for i in range(6))
        row = lambda p: p[l][None, :]
        head_params = (W["rw_k_k"][l], W["rw_k_a"][l], W["rw_r_k"][l].reshape(-1), W["rw_lnx_g"][l], W["rw_lnx_b"][l])
        pre_args = (x, sh1, sc1, row(W["norm_mix_g"]), W["w_in_bf"], l, shift0[l][:, None, :], conv0[l],
                    row(W["mu_shift"]), W["conv_w"][l], row(W["conv_b"]), row(W["conv_ln_g"]), row(W["conv_ln_b"]),
                    row(W["rw_w0"]), W["lora_bf"][l], row(W["rw_a0"]), W["g_up_bf"][l])
        post = (gt1, sh2, sc2, row(W["norm_ffn_g"]), W["w_out_bf"], l, W["wr_hi"][l], W["wr_lo"][l], W["br"][l])
        if wkv0 is None:
            x1, h2t, route, counts, convn, shiftn, s_l = _mixer_pre(
                *pre_args, wkv_params=[p[None, :] for p in head_params], post=post)
            states.append(s_l)
        else:
            yag, gb, r, lw, k, v, a, convn, shiftn = _mixer_pre(*pre_args)
            s0 = wkv0[l].transpose(2, 3, 0, 1).reshape(HEAD_DIM, HEAD_DIM, bsz * n_heads)
            y_l, s_l = _wkv_scan(*(_to_lanes(z, n_heads) for z in (r, lw, k, v, a)),
                                 *(_head_param(p, n_heads) for p in head_params), s0)
            states.append(s_l.reshape(HEAD_DIM, HEAD_DIM, bsz, n_heads).transpose(2, 3, 0, 1))
            x1, h2t, route, counts = _mixer_post(x, yag, gb, _from_lanes(y_l, bsz, n_heads), *post)
        x = _moe_routed(h2t, route, counts[0], W["moe_w_gate"], W["moe_w_up"], W["moe_w_down"], l, x1, gt2,
                        W["final_norm_g"][None, :] if l == depth - 1 else None)
        bufs.append(convn)
        shifts.append(shiftn[:, 0, :])
    return x, jnp.stack(bufs), jnp.stack(shifts), jnp.stack(states)


def kernel(x_prompt, x_sample, state_conv, state_shift, state_wkv, c_prompt, c_sample, norm_mix_g, norm_ffn_g, final_norm_g, ada_w, ada_b, w_in, mu_shift, conv_w, conv_b, conv_ln_g, conv_ln_b, rw_w0, rw_w_up, rw_a0, rw_a_up, rw_g_up, rw_k_k, rw_k_a, rw_r_k, rw_lnx_g, rw_lnx_b, w_out, moe_w_group, moe_b_group, moe_w_expert, moe_b_expert, moe_w_gate, moe_w_up, moe_w_down):
    depth, d, _ = w_in.shape
    zw = jnp.zeros_like(rw_w_up)
    lora = jnp.concatenate([jnp.concatenate([rw_w_up, zw], axis=2), jnp.concatenate([zw, rw_a_up], axis=2)], axis=1)
    pad = ROUTER_LANES - N_EXPERTS - N_GROUPS
    wr = jnp.concatenate([moe_w_expert, moe_w_group, jnp.zeros((depth, d, pad), F32)], axis=2)
    wr_hi = wr.astype(BF16)
    br = jnp.concatenate([moe_b_expert, moe_b_group, jnp.zeros((depth, pad), F32)], axis=1)[:, None, :]
    W = dict(norm_mix_g=norm_mix_g, norm_ffn_g=norm_ffn_g, final_norm_g=final_norm_g, ada_w=ada_w, ada_b=ada_b,
             w_in_bf=w_in.astype(BF16), mu_shift=mu_shift, conv_w=conv_w, conv_b=conv_b, conv_ln_g=conv_ln_g,
             conv_ln_b=conv_ln_b, rw_w0=rw_w0, lora_bf=lora.astype(BF16), rw_a0=rw_a0, g_up_bf=rw_g_up.astype(BF16),
             rw_k_k=rw_k_k, rw_k_a=rw_k_a, rw_r_k=rw_r_k, rw_lnx_g=rw_lnx_g, rw_lnx_b=rw_lnx_b,
             w_out_bf=w_out.astype(BF16), wr_hi=wr_hi, wr_lo=(wr - wr_hi.astype(F32)).astype(BF16), br=br,
             moe_w_gate=moe_w_gate, moe_w_up=moe_w_up, moe_w_down=moe_w_down)
    bp = x_prompt.shape[0]
    conv0 = jnp.zeros((depth, bp, CONV_BUF, d), x_prompt.dtype)
    shift0 = jnp.zeros((depth, bp, d), x_prompt.dtype)
    y_p, conv_p, shift_p, wkv_p = _trunk(x_prompt, c_prompt, shift0, conv0, None, W)
    y_s, conv_s, shift_s, wkv_s = _trunk(x_sample, c_sample, state_shift, state_conv, state_wkv, W)
    return (y_p, y_s, conv_p, shift_p, wkv_p.astype(state_wkv.dtype), conv_s, shift_s, wkv_s.astype(state_wkv.dtype))
```

```python
import functools

import jax
import jax.numpy as jnp
from jax import lax
from jax.experimental import pallas as pl
from jax.experimental.pallas import tpu as pltpu

F32 = jnp.float32
BF16 = jnp.bfloat16

HEAD_DIM = 64
CONV_WIDTH = 31
CONV_BUF = CONV_WIDTH - 1
LORA_W = 64
LORA_A = 64
N_GROUPS = 4
EXPERTS_PER_GROUP = 8
N_EXPERTS = N_GROUPS * EXPERTS_PER_GROUP
RMS_EPS = 1e-6
LN_EPS = 1e-5
GN_EPS = 64e-5

V7X_LANES = 128
V7X_SUBLANES = 8
V7X_VMEM_LIMIT_BYTES = 56 * 1024 * 1024

ROW_TILE = 256
SHORT_SEQ_ROWS = 128
HIST_ROWS = 32
HIST_PAD = HIST_ROWS - CONV_BUF
CONV_LANE_CHUNK = 256
WKV_CHUNK = 64
WKV_GROUP_HEADS = 4
WKV_GROUP = WKV_GROUP_HEADS * HEAD_DIM
WKV_INTERLEAVE_PER_MATMUL = 2
ROUTER_LANES = V7X_LANES
ROUTE_E1, ROUTE_E2, ROUTE_W1, ROUTE_W2, ROUTE_R1, ROUTE_R2 = 0, 1, 2, 3, 4, 5
MOE_ROW_TILE = 256
MOE_ROW_TILE_SMALL = 64
MOE_SCATTER_TOKENS = 256
MOE_COMBINE_TOKENS = 128


def _dot(a, b):
    return jnp.dot(a, b, preferred_element_type=F32)


def _sigmoid(x):
    return 1.0 / (1.0 + jnp.exp(-x))


def _rmsnorm(x, g):
    return x * lax.rsqrt(jnp.mean(x * x, axis=-1, keepdims=True) + RMS_EPS) * g


def _params(sem):
    return pltpu.CompilerParams(dimension_semantics=sem, vmem_limit_bytes=V7X_VMEM_LIMIT_BYTES)


def _row_tiling(bsz, t):
    tm = min(t, ROW_TILE)
    bb = max(1, min(bsz, SHORT_SEQ_ROWS // tm))
    assert t % tm == 0 and bsz % bb == 0 and tm % V7X_SUBLANES == 0
    assert bb == 1 or tm == t
    return bb, tm


def _per_seq(ref, bb, n, d):
    p = ref[...]
    if bb == 1:
        return p[0]
    return jnp.broadcast_to(p, (bb, n, d)).reshape(bb * n, d)


def _full_spec(a):
    return pl.BlockSpec(a.shape, lambda b, i: (0,) * a.ndim)


def _moe_residual_operands(res, bb, tm):
    moe, gate = res
    d = moe.shape[-1]
    return ([moe, gate], [pl.BlockSpec((bb, tm, d), lambda b, i: (b, i, 0)),
                          pl.BlockSpec((bb, 1, d), lambda b, i: (b, 0, 0))])


def _ada_kernel(c_ref, w_ref, b_ref, o_ref):
    c = c_ref[...]
    s = c * _sigmoid(c)
    o_ref[0] = _dot(s.astype(BF16), w_ref[0].astype(BF16)) + b_ref[0]


def _modulation(c, ada_w, ada_b):
    depth, d, n = ada_w.shape
    bsz = c.shape[0]
    tn = 1536
    return pl.pallas_call(
        _ada_kernel,
        out_shape=jax.ShapeDtypeStruct((depth, bsz, n), F32),
        grid=(depth, n // tn),
        in_specs=[
            pl.BlockSpec((bsz, d), lambda l, j: (0, 0)),
            pl.BlockSpec((1, d, tn), lambda l, j: (l, 0, j)),
            pl.BlockSpec((1, 1, tn), lambda l, j: (l, 0, j)),
        ],
        out_specs=pl.BlockSpec((1, bsz, tn), lambda l, j: (l, 0, j)),
        compiler_params=_params(("arbitrary", "arbitrary")),
        name="ada_modulation",
    )(c, ada_w, ada_b.reshape(depth, 1, n))


def _mixer_pre_kernel(has_res, fuse_wkv, bb, tm, d, *refs):
    if has_res:
        xa_ref, moe_ref, gtp_ref, *refs = refs
    else:
        xa_ref, *refs = refs
    (sh_ref, sc_ref, g_ref, win_ref, shift0_ref, conv0_ref, mu_ref, cw_ref, cb_ref, clg_ref, clb_ref,
     w0_ref, lora_ref, a0_ref, gup_ref, *refs) = refs
    if fuse_wkv:
        *wkv_params, ones_ref, tri_ref = refs[:7]
        refs = refs[7:]
    if has_res:
        xres_ref, *refs = refs
    if fuse_wkv:
        (merged_ref, convn_ref, shiftn_ref, sout_ref,
         ubuf_ref, z_ref, carry_ref, shifted_ref, s_ref) = refs
    else:
        (yag_ref, gb_ref, r_ref, lw_ref, k_ref, v_ref, a_ref, convn_ref, shiftn_ref,
         ubuf_ref, z_ref, carry_ref, shifted_ref) = refs
    c_plain = 4 * d
    rows = bb * tm
    t = pl.program_id(1)
    nt = pl.num_programs(1)

    @pl.when(t == 0)
    def _():
        hp = _per_seq(shift0_ref, bb, V7X_SUBLANES, d)
        hp = jnp.broadcast_to(hp, (bb * V7X_SUBLANES, d)).astype(BF16)
        carry_ref[...] = _dot(hp, win_ref[:, c_plain:])
        ubuf_ref[:, pl.ds(0, HIST_PAD), :] = jnp.zeros((bb, HIST_PAD, d), F32)
        ubuf_ref[:, pl.ds(HIST_PAD, CONV_BUF), :] = conv0_ref[...]
        if fuse_wkv:
            s_ref[...] = jnp.zeros_like(s_ref)

    x = xa_ref[...].reshape(rows, d)
    if has_res:
        x = x + _per_seq(gtp_ref, bb, tm, d) * moe_ref[...].reshape(rows, d)
        xres_ref[...] = x.reshape(bb, tm, d)
    h = _rmsnorm(x, g_ref[...]) * (1.0 + _per_seq(sc_ref, bb, tm, d)) + _per_seq(sh_ref, bb, tm, d)
    shiftn_ref[...] = h.reshape(bb, tm, d)[:, tm - 1:tm, :]
    hb = h.astype(BF16)
    cur = _dot(hb, win_ref[:, c_plain:])
    proj = _dot(hb, win_ref[:, :c_plain])

    prev = pltpu.roll(cur, 1, 0)
    row = lax.broadcasted_iota(jnp.int32, cur.shape, 0)
    if bb == 1:
        prev = jnp.where(row == 0, carry_ref[pl.ds(0, 1), :], prev)
        carry_ref[pl.ds(0, 1), :] = cur[tm - 1:tm, :]
    else:
        prev = jnp.where((row & (tm - 1)) == 0, carry_ref[...], prev)
    rw = cur + (prev - cur) * mu_ref[...]

    r = rw[:, 0:d]
    k = rw[:, d:2 * d]
    v = rw[:, 2 * d:3 * d]
    xwa = rw[:, 3 * d:3 * d + LORA_W + LORA_A]
    xg = rw[:, 3 * d + LORA_W + LORA_A:]
    lane = lax.broadcasted_iota(jnp.int32, xwa.shape, 1)
    lhs = jnp.where(lane < LORA_W, jnp.tanh(xwa), xwa)
    za = _dot(lhs.astype(BF16), lora_ref[...])
    log_decay = -jnp.exp(-0.5) * _sigmoid(w0_ref[...] + za[:, 0:d])
    a = _sigmoid(a0_ref[...] + za[:, d:])
    gate = _dot(_sigmoid(xg).astype(BF16), gup_ref[...])
    blk = (bb, tm, d)

    u = proj[:, 0:d] * _sigmoid(proj[:, d:2 * d])
    ubuf_ref[:, pl.ds(HIST_ROWS, tm), :] = u.reshape(blk)
    rb = min(tm, 32)
    lc = CONV_LANE_CHUNK
    span = tm + HIST_ROWS - V7X_SUBLANES

    def conv_shift(b, c0):
        for s in range(1, V7X_SUBLANES):
            shifted_ref[s - 1] = ubuf_ref[b, pl.ds(s, span), c0:c0 + lc]

    def conv_rows(b, c0, r0):
        acc = jnp.broadcast_to(cb_ref[:, c0:c0 + lc], (rb, lc))
        for tap in range(CONV_WIDTH):
            s = (HIST_PAD + tap) % V7X_SUBLANES
            base = r0 + HIST_PAD + tap - s
            if s == 0:
                src = ubuf_ref[b, pl.ds(base, rb), c0:c0 + lc]
            else:
                src = shifted_ref[s - 1, pl.ds(base, rb), :]
            acc = acc + cw_ref[pl.ds(tap, 1), c0:c0 + lc] * src
        z_ref[pl.ds(b * tm + r0, rb), c0:c0 + lc] = acc

    conv_work = []
    for c0 in range(0, d, lc):
        for b in range(bb):
            conv_work.append(functools.partial(conv_shift, b, c0))
            conv_work += [functools.partial(conv_rows, b, c0, r0) for r0 in range(0, tm, rb)]

    if fuse_wkv:
        y = _wkv_chunk_math(tm, d, r, log_decay, k, v, a, wkv_params, ones_ref, tri_ref, s_ref, conv_work)
    else:
        r_ref[...] = r.reshape(blk)
        k_ref[...] = k.reshape(blk)
        v_ref[...] = v.reshape(blk)
        lw_ref[...] = log_decay.reshape(blk)
        a_ref[...] = a.reshape(blk)
    gb = _sigmoid(proj[:, 3 * d:4 * d]) * gate

    for work in conv_work:
        work()
    z = z_ref[...]
    mu = jnp.mean(z, axis=-1, keepdims=True)
    zc = z - mu
    var = jnp.mean(zc * zc, axis=-1, keepdims=True)
    zn = zc * lax.rsqrt(var + LN_EPS) * clg_ref[...] + clb_ref[...]
    yag = _sigmoid(proj[:, 2 * d:3 * d]) * (zn * _sigmoid(zn))
    if fuse_wkv:
        merged_ref[...] = (yag + gb * y).reshape(blk)
    else:
        yag_ref[...] = yag.reshape(blk)
        gb_ref[...] = gb.reshape(blk)

    convn_ref[...] = ubuf_ref[:, pl.ds(tm + HIST_PAD, CONV_BUF), :]
    if tm >= HIST_ROWS:
        ubuf_ref[:, pl.ds(0, HIST_ROWS), :] = ubuf_ref[:, pl.ds(tm, HIST_ROWS), :]

    if fuse_wkv:
        @pl.when(t == nt - 1)
        def _():
            for hd in range(d // HEAD_DIM):
                g, hs = divmod(hd, WKV_GROUP_HEADS)
                hs = slice(hs * HEAD_DIM, (hs + 1) * HEAD_DIM)
                sout_ref[0, hd] = s_ref[g, hs, hs]


def _mixer_pre(xa, res, sh, sc, g, win_bf, layer, shift0, conv0, mu, cw, cb, clg, clb, w0, lora_bf, a0, gup_bf,
               wkv_params=None):
    bsz, t, d = xa.shape
    bb, tm = _row_tiling(bsz, t)
    assert t == tm or tm >= HIST_ROWS
    assert bb == 1 or tm == V7X_SUBLANES
    fuse_wkv = wkv_params is not None
    c_in = win_bf.shape[2]
    c_rwkv = c_in - 4 * d
    tile = pl.BlockSpec((bb, tm, d), lambda b, i: (b, i, 0))
    perb = pl.BlockSpec((bb, 1, d), lambda b, i: (b, 0, 0))
    hist = pl.BlockSpec((bb, CONV_BUF, d), lambda b, i: (b, 0, 0))
    ins, specs = [xa], [tile]
    n_act = 1 if fuse_wkv else 7
    if res is not None:
        res_ins, res_specs = _moe_residual_operands(res, bb, tm)
        ins += res_ins
        specs += res_specs
        n_act += 1
    ins += [sh, sc, g, win_bf, shift0, conv0, mu, cw, cb, clg, clb, w0, lora_bf, a0, gup_bf]
    specs += [perb, perb, _full_spec(g),
              pl.BlockSpec((None, d, c_in), lambda b, i: (layer, 0, 0), pipeline_mode=pl.Buffered(1)),
              perb, hist] + [_full_spec(p) for p in (mu, cw, cb, clg, clb, w0, lora_bf, a0, gup_bf)]
    act = jax.ShapeDtypeStruct((bsz, t, d), F32)
    out_shape = [act] * n_act + [jax.ShapeDtypeStruct((bsz, CONV_BUF, d), F32), jax.ShapeDtypeStruct((bsz, 1, d), F32)]
    out_specs = [tile] * n_act + [hist, perb]
    scratch = [pltpu.VMEM((bb, tm + HIST_ROWS, d), F32), pltpu.VMEM((bb * tm, d), F32),
               pltpu.VMEM((bb * V7X_SUBLANES, c_rwkv), F32),
               pltpu.VMEM((V7X_SUBLANES - 1, tm + HIST_ROWS - V7X_SUBLANES, CONV_LANE_CHUNK), F32)]
    if fuse_wkv:
        assert bb == 1 and tm % WKV_CHUNK == 0 and d % WKV_GROUP == 0
        n_heads = d // HEAD_DIM
        consts = _wkv_constants()
        ins += list(wkv_params) + list(consts)
        specs += [_full_spec(p) for p in wkv_params] + [_full_spec(c) for c in consts]
        out_shape.append(jax.ShapeDtypeStruct((bsz, n_heads, HEAD_DIM, HEAD_DIM), F32))
        out_specs.append(pl.BlockSpec((1, n_heads, HEAD_DIM, HEAD_DIM), lambda b, i: (b, 0, 0, 0)))
        scratch.append(pltpu.VMEM((d // WKV_GROUP, WKV_GROUP, WKV_GROUP), F32))
    return pl.pallas_call(
        functools.partial(_mixer_pre_kernel, res is not None, fuse_wkv, bb, tm, d),
        out_shape=out_shape,
        grid=(bsz // bb, t // tm),
        in_specs=specs,
        out_specs=out_specs,
        scratch_shapes=scratch,
        compiler_params=_params(("arbitrary", "arbitrary")),
        name="mixer_pre",
    )(*ins)


def _wkv_kernel(tc, r_ref, lw_ref, k_ref, v_ref, a_ref, kkp_ref, kap_ref, rkp_ref, lng_ref, lnb_ref, s0_ref,
                y_ref, s_ref, o_ref):
    n = HEAD_DIM

    @pl.when(pl.program_id(1) == 0)
    def _():
        s_ref[...] = s0_ref[...]

    def step(tt, carry):
        r_t = r_ref[tt]
        k_t = k_ref[tt]
        a_t = a_ref[tt]
        w_t = jnp.exp(lw_ref[tt])
        kk = k_t * kkp_ref[...]
        kk = kk * lax.rsqrt(jnp.sum(kk * kk, axis=0, keepdims=True) + 1e-12)
        b_t = kk * a_t
        k2 = k_t * (1.0 + (a_t - 1.0) * kap_ref[...])

        def row(i, c):
            si = s_ref[i]
            sa = jnp.sum(si * kk, axis=0, keepdims=True)
            vi = v_ref[tt, pl.ds(i, 1), :]
            sn = si * w_t - sa * b_t + vi * k2
            s_ref[i] = sn
            o_ref[pl.ds(i, 1), :] = jnp.sum(sn * r_t, axis=0, keepdims=True)
            return c

        lax.fori_loop(0, n, row, 0, unroll=8)
        o = o_ref[...]
        mu = jnp.mean(o, axis=0, keepdims=True)
        oc = o - mu
        var = jnp.mean(oc * oc, axis=0, keepdims=True)
        yn = oc * lax.rsqrt(var + GN_EPS) * lng_ref[...] + lnb_ref[...]
        bonus = jnp.sum(r_t * k2 * rkp_ref[...], axis=0, keepdims=True) * v_ref[tt]
        y_ref[tt] = yn + bonus
        return carry

    lax.fori_loop(0, tc, step, 0)


def _wkv_scan(r, w, k, v, a, kkp, kap, rkp, lng, lnb, s0):
    t, n, nl = r.shape
    tc = min(t, 32)
    lb = V7X_LANES
    seq = pl.BlockSpec((tc, n, lb), lambda l, i: (i, 0, l))
    par = pl.BlockSpec((n, lb), lambda l, i: (0, 0))
    st = pl.BlockSpec((n, n, lb), lambda l, i: (0, 0, l))
    return pl.pallas_call(
        functools.partial(_wkv_kernel, tc),
        out_shape=[jax.ShapeDtypeStruct((t, n, nl), F32), jax.ShapeDtypeStruct((n, n, nl), F32)],
        grid=(nl // lb, t // tc),
        in_specs=[seq] * 5 + [par] * 5 + [st],
        out_specs=[seq, st],
        scratch_shapes=[pltpu.VMEM((n, lb), F32)],
        compiler_params=_params(("arbitrary", "arbitrary")),
        name="wkv_scan",
    )(r, w, k, v, a, kkp, kap, rkp, lng, lnb, s0)


def _bmm(a, b):
    return jnp.einsum("gck,gkn->gcn", a, b, preferred_element_type=F32)


def _bmm_nt(a, b):
    return jnp.einsum("gck,gnk->gcn", a, b, preferred_element_type=F32)


def _bmm_tn(a, b):
    return jnp.einsum("gtc,gtn->gcn", a, b, preferred_element_type=F32)


def _split_bf16(x):
    hi = x.astype(BF16)
    return hi, (x - hi.astype(F32)).astype(BF16)


def _wkv_constants():
    li = lax.broadcasted_iota(jnp.int32, (WKV_GROUP, WKV_GROUP), 0) // HEAD_DIM
    lj = lax.broadcasted_iota(jnp.int32, (WKV_GROUP, WKV_GROUP), 1) // HEAD_DIM
    ti = lax.broadcasted_iota(jnp.int32, (WKV_CHUNK, WKV_CHUNK), 0)
    tj = lax.broadcasted_iota(jnp.int32, (WKV_CHUNK, WKV_CHUNK), 1)
    return (li == lj).astype(BF16), (ti >= tj).astype(BF16)


def _wkv_chunk_math(tt, d, r, lw, k, v, a, params, ones_ref, tri_ref, s_ref, interleaved):
    c = WKV_CHUNK
    gw = WKV_GROUP
    ng = d // gw
    nc = tt // c
    n_items = nc * ng
    kkp_ref, kap_ref, rkp_ref, lng_ref, lnb_ref = params

    lane = lax.broadcasted_iota(jnp.int32, (1, c, gw), 2)
    rowi = lax.broadcasted_iota(jnp.int32, (1, c, gw), 1)
    head_of_lane = lax.shift_right_logical(lane, 6)
    hmask = [head_of_lane == h for h in range(WKV_GROUP_HEADS)]
    tcol = lane & (c - 1)
    strict = tcol < rowi
    incl = tcol <= rowi
    bl = lax.broadcasted_iota(jnp.int32, (1, gw, gw), 2)
    br = lax.broadcasted_iota(jnp.int32, (1, gw, gw), 1)
    blockmask = lax.shift_right_logical(bl, 6) == lax.shift_right_logical(br, 6)

    def items(x):
        x = x.reshape(nc, c, d)
        return jnp.stack([x[:, :, g * gw:(g + 1) * gw] for g in range(ng)], axis=1).reshape(n_items, c, gw)

    def per_group(ref):
        p = jnp.stack([ref[:, g * gw:(g + 1) * gw] for g in range(ng)], axis=0)
        return jnp.broadcast_to(p[None], (nc, ng, 1, gw)).reshape(n_items, 1, gw)

    def bd(xf):
        return jnp.concatenate([jnp.where(hmask[h], xf, 0.0) for h in range(WKV_GROUP_HEADS)], axis=1).astype(BF16)

    def segsum(x):
        shp = x.shape
        hi, lo = _split_bf16(x.reshape(-1, gw))
        return (_dot(hi, ones_ref[...]) + _dot(lo, ones_ref[...])).reshape(shp)

    def filled(y):
        for _ in range(WKV_INTERLEAVE_PER_MATMUL):
            if interleaved:
                interleaved.pop(0)()
        return y

    def bmm(x, y):
        return filled(_bmm(x, y))

    def bmm_nt(x, y):
        return filled(_bmm_nt(x, y))

    def bmm_tn(x, y):
        return filled(_bmm_tn(x, y))

    r, lw, k, v, a = (items(x) for x in (r, lw, k, v, a))
    kk = k * per_group(kkp_ref)
    kk = kk * lax.rsqrt(segsum(kk * kk) + 1e-12)
    b = kk * a
    k2 = k * (1.0 + (a - 1.0) * per_group(kap_ref))
    lw_hi, lw_lo = _split_bf16(lw)
    tri = jnp.broadcast_to(tri_ref[...][None], (n_items, c, c))
    cum = bmm(tri, lw_hi) + bmm(tri, lw_lo)
    cl = cum[:, c - 1:c, :]
    at = -(kk * jnp.exp(cum - lw))
    rt = r * jnp.exp(cum)
    winv = jnp.exp(-cum)
    wrem = jnp.exp(cl - cum)
    wc = jnp.exp(cl)
    lhs = jnp.concatenate([at, rt], axis=1).astype(BF16)
    rhs = jnp.concatenate([bd(b * winv), bd(k2 * winv)], axis=1)
    sc = bmm_nt(lhs, rhs)
    l_ab = jnp.where(strict, sc[:, 0:c, 0:gw], 0.0)
    l_ak = jnp.where(strict, sc[:, 0:c, gw:], 0.0)
    l_rb = jnp.where(incl, sc[:, c:, 0:gw], 0.0)
    l_rk = jnp.where(incl, sc[:, c:, gw:], 0.0)
    pk = l_ab
    tm = l_ab
    pbd = bd(pk)
    for _ in range(c.bit_length() - 2):
        pk = bmm(pk.astype(BF16), pbd)
        pbd = bd(pk)
        tm = tm + pk + bmm(tm.astype(BF16), pbd)
    tmb = tm.astype(BF16)
    vbd = bd(v)
    ah = at + bmm(tmb, bd(at))
    x1 = bmm(l_ak.astype(BF16), vbd)
    gm = x1 + bmm(tmb, bd(x1))
    l_rbb = l_rb.astype(BF16)
    rh = (rt + bmm(l_rbb, bd(ah))).astype(BF16)
    o_ind = bmm(jnp.concatenate([l_rbb, l_rk.astype(BF16)], axis=2), jnp.concatenate([bd(gm), vbd], axis=1))
    bb = (b * wrem).astype(BF16)
    kb = (k2 * wrem).astype(BF16)
    m = jnp.where(blockmask, bmm_tn(ah.astype(BF16), bb), 0.0).astype(BF16)
    n = jnp.where(blockmask, bmm_tn(jnp.concatenate([gm, v], axis=1).astype(BF16),
                                    jnp.concatenate([bb, kb], axis=1)), 0.0)

    s = s_ref[...]
    outs = []
    for ci in range(nc):
        sl = slice(ci * ng, (ci + 1) * ng)
        sb = s.astype(BF16)
        outs.append(bmm_nt(rh[sl], sb) + o_ind[sl])
        s = s * wc[sl] + bmm(sb, m[sl]) + n[sl]
    s_ref[...] = s

    inv_n = 1.0 / HEAD_DIM
    o = jnp.stack(outs, axis=0).reshape(n_items, c, gw)
    mu = segsum(o) * inv_n
    oc = o - mu
    var = segsum(oc * oc) * inv_n
    yn = oc * lax.rsqrt(var + GN_EPS) * per_group(lng_ref) + per_group(lnb_ref)
    bonus = segsum(r * k2 * per_group(rkp_ref)) * v
    y = (yn + bonus).reshape(nc, ng, c, gw)
    return jnp.concatenate([y[:, g].reshape(tt, gw) for g in range(ng)], axis=1)


def _mixer_post_kernel(n_mix, bb, tm, d, x_ref, *refs):
    mix_refs = refs[:n_mix]
    (gt1_ref, sh_ref, sc_ref, g_ref, wout_ref, wrh_ref, wrl_ref, br_ref, tri_ref,
     x1_ref, h2_ref, route_ref, counts_ref, cnt_ref) = refs[n_mix:]
    rows = bb * tm

    @pl.when((pl.program_id(0) == 0) & (pl.program_id(1) == 0))
    def _():
        cnt_ref[...] = jnp.zeros_like(cnt_ref)

    x = x_ref[...].reshape(rows, d)
    if n_mix == 1:
        merged = mix_refs[0][...].reshape(rows, d)
    else:
        yag_ref, gb_ref, y_ref = mix_refs
        merged = (yag_ref[...] + gb_ref[...] * y_ref[...]).reshape(rows, d)
    x1 = x + _per_seq(gt1_ref, bb, tm, d) * _dot(merged.astype(BF16), wout_ref[...])
    x1_ref[...] = x1.reshape(bb, tm, d)
    h2 = _rmsnorm(x1, g_ref[...]) * (1.0 + _per_seq(sc_ref, bb, tm, d)) + _per_seq(sh_ref, bb, tm, d)
    h2_ref[...] = h2.reshape(bb, tm, V7X_SUBLANES, d // V7X_SUBLANES)
    h_hi = h2.astype(BF16)
    h_lo = (h2 - h_hi.astype(F32)).astype(BF16)
    logits = _dot(h_hi, wrh_ref[...]) + (_dot(h_lo, wrh_ref[...]) + _dot(h_hi, wrl_ref[...])) + br_ref[...]

    lane_i = lax.broadcasted_iota(jnp.int32, logits.shape, 1)
    lane = lane_i.astype(F32)
    grp_of_lane = lax.shift_right_logical(lane_i, 3).astype(F32)
    neg = -jnp.inf
    none = 1e9
    is_g = (lane_i >= N_EXPERTS) & (lane_i < N_EXPERTS + N_GROUPS)
    glv = jnp.where(is_g, logits, neg)
    gmax = jnp.max(glv, axis=-1, keepdims=True)
    gidx = jnp.min(jnp.where(is_g & (glv == gmax), lane - N_EXPERTS, none), axis=-1, keepdims=True)
    pg = 1.0 / jnp.sum(jnp.where(is_g, jnp.exp(glv - gmax), 0.0), axis=-1, keepdims=True)
    in_grp = (lane_i < N_EXPERTS) & (grp_of_lane == gidx)
    ev = jnp.where(in_grp, logits, neg)
    v1 = jnp.max(ev, axis=-1, keepdims=True)
    i1 = jnp.min(jnp.where(in_grp & (ev == v1), lane, none), axis=-1, keepdims=True)
    ev2 = jnp.where(lane == i1, neg, ev)
    v2 = jnp.max(ev2, axis=-1, keepdims=True)
    i2 = jnp.min(jnp.where(in_grp & (ev2 == v2) & (lane != i1), lane, none), axis=-1, keepdims=True)
    e2 = jnp.exp(v2 - v1)
    den = 1.0 + e2
    hit = ((lane == i1) | (lane == i2)).astype(F32)
    before = cnt_ref[pl.ds(0, 1), :] + _dot(tri_ref[...], hit.astype(BF16))
    r1 = jnp.sum(jnp.where(lane == i1, before, 0.0), axis=-1, keepdims=True)
    r2 = jnp.sum(jnp.where(lane == i2, before, 0.0), axis=-1, keepdims=True)
    counts = cnt_ref[pl.ds(0, 1), :] + jnp.sum(hit, axis=0, keepdims=True)
    cnt_ref[pl.ds(0, 1), :] = counts
    counts_ref[...] = jnp.broadcast_to(counts, counts_ref.shape)
    route = (jnp.where(lane_i == ROUTE_E1, i1, 0.0) + jnp.where(lane_i == ROUTE_E2, i2, 0.0)
             + jnp.where(lane_i == ROUTE_W1, (1.0 / den) * pg, 0.0) + jnp.where(lane_i == ROUTE_W2, (e2 / den) * pg, 0.0)
             + jnp.where(lane_i == ROUTE_R1, r1, 0.0) + jnp.where(lane_i == ROUTE_R2, r2, 0.0))
    route_ref[...] = route.reshape(bb, tm, ROUTER_LANES)


def _mixer_post(x, mix, gt1, sh, sc, g, wout_bf, wr_hi, wr_lo, br):
    bsz, t, d = x.shape
    bb, tm = _row_tiling(bsz, t)
    rows = bb * tm
    tile = pl.BlockSpec((bb, tm, d), lambda b, i: (b, i, 0))
    perb = pl.BlockSpec((bb, 1, d), lambda b, i: (b, 0, 0))
    ri = lax.broadcasted_iota(jnp.int32, (rows, rows), 0)
    ci = lax.broadcasted_iota(jnp.int32, (rows, rows), 1)
    tri = (ci < ri).astype(BF16)
    ins = [x, *mix, gt1, sh, sc, g, wout_bf, wr_hi, wr_lo, br, tri]
    specs = [tile] * (1 + len(mix)) + [perb, perb, perb] + [_full_spec(p) for p in (g, wout_bf, wr_hi, wr_lo, br, tri)]
    token_tiles = (bb, tm, V7X_SUBLANES, d // V7X_SUBLANES)
    return pl.pallas_call(
        functools.partial(_mixer_post_kernel, len(mix), bb, tm, d),
        out_shape=[jax.ShapeDtypeStruct((bsz, t, d), F32),
                   jax.ShapeDtypeStruct((bsz, t, V7X_SUBLANES, d // V7X_SUBLANES), F32),
                   jax.ShapeDtypeStruct((bsz, t, ROUTER_LANES), F32),
                   jax.ShapeDtypeStruct((V7X_SUBLANES, ROUTER_LANES), F32)],
        grid=(bsz // bb, t // tm),
        in_specs=specs,
        out_specs=[tile, pl.BlockSpec(token_tiles, lambda b, i: (b, i, 0, 0)),
                   pl.BlockSpec((bb, tm, ROUTER_LANES), lambda b, i: (b, i, 0)),
                   pl.BlockSpec((V7X_SUBLANES, ROUTER_LANES), lambda b, i: (0, 0))],
        scratch_shapes=[pltpu.VMEM((V7X_SUBLANES, ROUTER_LANES), F32)],
        compiler_params=_params(("arbitrary", "arbitrary")),
        name="mixer_post",
    )(*ins)


def _moe_tables(route, counts, tm):
    n = route.shape[0]
    nt = 2 * n // tm
    cnt = counts[:N_EXPERTS].astype(jnp.int32)
    ends = jnp.cumsum(cnt)
    starts = ends - cnt
    ids = jnp.arange(N_EXPERTS, dtype=jnp.int32)[None, :]

    def position(e_lane, r_lane):
        e = route[:, e_lane].astype(jnp.int32)
        return jnp.sum(jnp.where(e[:, None] == ids, starts[None, :], 0), axis=1) + route[:, r_lane].astype(jnp.int32)

    pos = jnp.concatenate([position(ROUTE_E1, ROUTE_R1), position(ROUTE_E2, ROUTE_R2)])
    bounds = jnp.sort(jnp.concatenate([jnp.arange(nt + 1, dtype=jnp.int32) * tm, ends]))
    lo_abs, hi_abs = bounds[:-1], bounds[1:]
    tile = jnp.minimum(lo_abs // tm, nt - 1)
    nonzero = hi_abs > lo_abs
    expert = jnp.minimum(jnp.sum((ends[None, :] <= lo_abs[:, None]).astype(jnp.int32), axis=1), N_EXPERTS - 1)
    expert = jnp.maximum(lax.cummax(jnp.where(nonzero, expert, -1)), 0)
    prev_tile = jnp.concatenate([jnp.full((1,), -1, jnp.int32), lax.cummax(jnp.where(nonzero, tile, -1))[:-1]])
    first = (nonzero & (tile > prev_tile)).astype(jnp.int32)
    return pos, tile, expert, lo_abs - tile * tm, hi_abs - tile * tm, first


def _moe_scatter_kernel(ts, n, pos_ref, h_ref, xs_hbm, sem):
    i = pl.program_id(0)

    def copies(start):
        def body(j, c):
            for s in range(2):
                cp = pltpu.make_async_copy(h_ref.at[j], xs_hbm.at[pos_ref[s * n + i * ts + j]], sem)
                if start:
                    cp.start(priority=s)
                else:
                    cp.wait()
            return c
        lax.fori_loop(0, ts, body, 0, unroll=8)

    copies(True)
    copies(False)


def _moe_scatter(h2t, pos):
    n = h2t.shape[0]
    ts = min(n, MOE_SCATTER_TOKENS)
    grid_spec = pltpu.PrefetchScalarGridSpec(
        num_scalar_prefetch=1, grid=(n // ts,),
        in_specs=[pl.BlockSpec((ts, V7X_SUBLANES, V7X_LANES), lambda i, p: (i, 0, 0))],
        out_specs=pl.BlockSpec(memory_space=pl.ANY),
        scratch_shapes=[pltpu.SemaphoreType.DMA(())])
    return pl.pallas_call(
        functools.partial(_moe_scatter_kernel, ts, n),
        out_shape=jax.ShapeDtypeStruct((2 * n, V7X_SUBLANES, V7X_LANES), F32),
        grid_spec=grid_spec, compiler_params=_params(("arbitrary",)), name="moe_scatter")(pos, h2t)


def _moe_experts_kernel(tm, d, tile_ref, exp_ref, lo_ref, hi_ref, first_ref, xs_ref, wg_ref, wu_ref, wd_ref, ys_ref):
    w = pl.program_id(0)
    lo = lo_ref[w]
    hi = hi_ref[w]

    @pl.when(hi > lo)
    def _():
        x = xs_ref[...].reshape(tm, d).astype(BF16)
        hg = _dot(x, wg_ref[0].astype(BF16))
        hu = _dot(x, wu_ref[0].astype(BF16))
        act = hg * _sigmoid(hg) * hu
        y = _dot(act.astype(BF16), wd_ref[0].astype(BF16))
        row = lax.broadcasted_iota(jnp.int32, (tm, 1), 0)
        y = jnp.where((row >= lo) & (row < hi), y, 0.0).reshape(tm, V7X_SUBLANES, V7X_LANES)

        @pl.when(first_ref[w] == 1)
        def _():
            ys_ref[...] = y

        @pl.when(first_ref[w] == 0)
        def _():
            ys_ref[...] += y


def _moe_experts(xs, tile, expert, lo, hi, first, wg, wu, wd, layer, tm):
    d = V7X_SUBLANES * V7X_LANES
    f = wg.shape[-1]
    row_tile = pl.BlockSpec((tm, V7X_SUBLANES, V7X_LANES), lambda w, t, e, *_: (t[w], 0, 0))
    grid_spec = pltpu.PrefetchScalarGridSpec(
        num_scalar_prefetch=5, grid=(tile.shape[0],),
        in_specs=[row_tile,
                  pl.BlockSpec((None, 1, d, f), lambda w, t, e, *_: (layer, e[w], 0, 0)),
                  pl.BlockSpec((None, 1, d, f), lambda w, t, e, *_: (layer, e[w], 0, 0)),
                  pl.BlockSpec((None, 1, f, d), lambda w, t, e, *_: (layer, e[w], 0, 0))],
        out_specs=row_tile)
    return pl.pallas_call(
        functools.partial(_moe_experts_kernel, tm, d),
        out_shape=jax.ShapeDtypeStruct(xs.shape, F32),
        grid_spec=grid_spec, compiler_params=_params(("arbitrary",)), name="moe_experts")(
            tile, expert, lo, hi, first, xs, wg, wu, wd)


def _moe_combine_kernel(tc, n, d, pos_ref, ys_hbm, route_ref, o_ref, gbuf, sem):
    i = pl.program_id(0)
    nt = pl.num_programs(0)
    slot = lax.rem(i, 2)

    def copies(blk, sl, start):
        def body(j, c):
            for s in range(2):
                cp = pltpu.make_async_copy(ys_hbm.at[pos_ref[s * n + blk * tc + j]], gbuf.at[sl, s, j], sem.at[sl])
                if start:
                    cp.start(priority=s)
                else:
                    cp.wait()
            return c
        lax.fori_loop(0, tc, body, 0, unroll=8)

    @pl.when(i == 0)
    def _():
        copies(0, 0, True)

    @pl.when(i + 1 < nt)
    def _():
        copies(i + 1, 1 - slot, True)

    copies(i, slot, False)
    route = route_ref[...]
    y0 = gbuf[slot, 0].reshape(tc, d)
    y1 = gbuf[slot, 1].reshape(tc, d)
    o_ref[...] = route[:, ROUTE_W1:ROUTE_W1 + 1] * y0 + route[:, ROUTE_W2:ROUTE_W2 + 1] * y1


def _moe_combine(ys, pos, route):
    n = route.shape[0]
    d = V7X_SUBLANES * V7X_LANES
    tc = min(n, MOE_COMBINE_TOKENS)
    grid_spec = pltpu.PrefetchScalarGridSpec(
        num_scalar_prefetch=1, grid=(n // tc,),
        in_specs=[pl.BlockSpec(memory_space=pl.ANY),
                  pl.BlockSpec((tc, ROUTER_LANES), lambda i, p: (i, 0))],
        out_specs=pl.BlockSpec((tc, d), lambda i, p: (i, 0)),
        scratch_shapes=[pltpu.VMEM((2, 2, tc, V7X_SUBLANES, V7X_LANES), F32), pltpu.SemaphoreType.DMA((2,))])
    return pl.pallas_call(
        functools.partial(_moe_combine_kernel, tc, n, d),
        out_shape=jax.ShapeDtypeStruct((n, d), F32),
        grid_spec=grid_spec, compiler_params=_params(("arbitrary",)), name="moe_combine")(pos, ys, route)


def _moe_routed(h2t, route, counts, wg, wu, wd, layer):
    n = h2t.shape[0]
    assert h2t.shape[1:] == (V7X_SUBLANES, V7X_LANES)
    tm = MOE_ROW_TILE if n >= 32 * MOE_ROW_TILE else MOE_ROW_TILE_SMALL
    pos, tile, expert, lo, hi, first = _moe_tables(route, counts, tm)
    xs = _moe_scatter(h2t, pos)
    ys = _moe_experts(xs, tile, expert, lo, hi, first, wg, wu, wd, layer, tm)
    return _moe_combine(ys, pos, route)


def _final_kernel(bb, tm, d, xa_ref, moe_ref, gtp_ref, g_ref, o_ref):
    rows = bb * tm
    x = xa_ref[...].reshape(rows, d) + _per_seq(gtp_ref, bb, tm, d) * moe_ref[...].reshape(rows, d)
    o_ref[...] = _rmsnorm(x, g_ref[...]).reshape(bb, tm, d)


def _final_norm(xa, res, g):
    bsz, t, d = xa.shape
    bb, tm = _row_tiling(bsz, t)
    tile = pl.BlockSpec((bb, tm, d), lambda b, i: (b, i, 0))
    res_ins, res_specs = _moe_residual_operands(res, bb, tm)
    return pl.pallas_call(
        functools.partial(_final_kernel, bb, tm, d),
        out_shape=jax.ShapeDtypeStruct((bsz, t, d), F32),
        grid=(bsz // bb, t // tm),
        in_specs=[tile] + res_specs + [_full_spec(g)],
        out_specs=tile,
        compiler_params=_params(("arbitrary", "arbitrary")),
        name="final_norm",
    )(xa, *res_ins, g)


def _to_lanes(z, n_heads):
    bsz, t, _ = z.shape
    return z.reshape(bsz, t, n_heads, HEAD_DIM).transpose(1, 3, 0, 2).reshape(t, HEAD_DIM, bsz * n_heads)


def _from_lanes(z, bsz, n_heads):
    t = z.shape[0]
    return z.reshape(t, HEAD_DIM, bsz, n_heads).transpose(2, 0, 3, 1).reshape(bsz, t, n_heads * HEAD_DIM)


def _head_param(p, n_heads):
    return jnp.tile(p.reshape(n_heads, HEAD_DIM).T, (1, V7X_LANES // n_heads))


def _trunk(x, c, shift0, conv0, wkv0, W):
    bsz, t, d = x.shape
    n_heads = d // HEAD_DIM
    depth = W["w_in_bf"].shape[0]
    mod = _modulation(c, W["ada_w"], W["ada_b"])
    bufs, shifts, states = [], [], []
    res = None
    xa = x
    for l in range(depth):
        sh1, sc1, gt1, sh2, sc2, gt2 = (mod[l, :, i * d:(i + 1) * d][:, None, :] for i in range(6))
        row = lambda p: p[l][None, :]
        head_params = (W["rw_k_k"][l], W["rw_k_a"][l], W["rw_r_k"][l].reshape(-1), W["rw_lnx_g"][l], W["rw_lnx_b"][l])
        pre = _mixer_pre(
            xa, res, sh1, sc1, row(W["norm_mix_g"]), W["w_in_bf"], l, shift0[l][:, None, :], conv0[l],
            row(W["mu_shift"]), W["conv_w"][l], row(W["conv_b"]), row(W["conv_ln_g"]), row(W["conv_ln_b"]),
            row(W["rw_w0"]), W["lora_bf"][l], row(W["rw_a0"]), W["g_up_bf"][l],
            wkv_params=[p[None, :] for p in head_params] if wkv0 is None else None)
        if res is not None:
            xa, *pre = pre
        if wkv0 is None:
            merged, convn, shiftn, s_l = pre
            mix = (merged,)
            states.append(s_l)
        else:
            yag, gb, r, lw, k, v, a, convn, shiftn = pre
            s0 = wkv0[l].transpose(2, 3, 0, 1).reshape(HEAD_DIM, HEAD_DIM, bsz * n_heads)
            y_l, s_l = _wkv_scan(*(_to_lanes(z, n_heads) for z in (r, lw, k, v, a)),
                                 *(_head_param(p, n_heads) for p in head_params), s0)
            mix = (yag, gb, _from_lanes(y_l, bsz, n_heads))
            states.append(s_l.reshape(HEAD_DIM, HEAD_DIM, bsz, n_heads).transpose(2, 3, 0, 1))
        x1, h2t, route, counts = _mixer_post(xa, mix, gt1, sh2, sc2, row(W["norm_ffn_g"]), W["w_out_bf"][l],
                                             W["wr_hi"][l], W["wr_lo"][l], W["br"][l])
        moe = _moe_routed(h2t.reshape(bsz * t, V7X_SUBLANES, V7X_LANES), route.reshape(bsz * t, ROUTER_LANES),
                          counts[0], W["moe_w_gate"], W["moe_w_up"], W["moe_w_down"], l)
        xa, res = x1, (moe.reshape(bsz, t, d), gt2)
        bufs.append(convn)
        shifts.append(shiftn[:, 0, :])
    y_out = _final_norm(xa, res, W["final_norm_g"][None, :])
    return y_out, jnp.stack(bufs), jnp.stack(shifts), jnp.stack(states)


def kernel(x_prompt, x_sample, state_conv, state_shift, state_wkv, c_prompt, c_sample, norm_mix_g, norm_ffn_g, final_norm_g, ada_w, ada_b, w_in, mu_shift, conv_w, conv_b, conv_ln_g, conv_ln_b, rw_w0, rw_w_up, rw_a0, rw_a_up, rw_g_up, rw_k_k, rw_k_a, rw_r_k, rw_lnx_g, rw_lnx_b, w_out, moe_w_group, moe_b_group, moe_w_expert, moe_b_expert, moe_w_gate, moe_w_up, moe_w_down):
    depth, d, _ = w_in.shape
    zw = jnp.zeros_like(rw_w_up)
    lora = jnp.concatenate([jnp.concatenate([rw_w_up, zw], axis=2), jnp.concatenate([zw, rw_a_up], axis=2)], axis=1)
    pad = ROUTER_LANES - N_EXPERTS - N_GROUPS
    wr = jnp.concatenate([moe_w_expert, moe_w_group, jnp.zeros((depth, d, pad), F32)], axis=2)
    wr_hi = wr.astype(BF16)
    br = jnp.concatenate([moe_b_expert, moe_b_group, jnp.zeros((depth, pad), F32)], axis=1)[:, None, :]
    W = dict(norm_mix_g=norm_mix_g, norm_ffn_g=norm_ffn_g, final_norm_g=final_norm_g, ada_w=ada_w, ada_b=ada_b,
             w_in_bf=w_in.astype(BF16), mu_shift=mu_shift, conv_w=conv_w, conv_b=conv_b, conv_ln_g=conv_ln_g,
             conv_ln_b=conv_ln_b, rw_w0=rw_w0, lora_bf=lora.astype(BF16), rw_a0=rw_a0, g_up_bf=rw_g_up.astype(BF16),
             rw_k_k=rw_k_k, rw_k_a=rw_k_a, rw_r_k=rw_r_k, rw_lnx_g=rw_lnx_g, rw_lnx_b=rw_lnx_b,
             w_out_bf=w_out.astype(BF16), wr_hi=wr_hi, wr_lo=(wr - wr_hi.astype(F32)).astype(BF16), br=br,
             moe_w_gate=moe_w_gate, moe_w_up=moe_w_up, moe_w_down=moe_w_down)
    bp = x_prompt.shape[0]
    conv0 = jnp.zeros((depth, bp, CONV_BUF, d), x_prompt.dtype)
    shift0 = jnp.zeros((depth, bp, d), x_prompt.dtype)
    y_p, conv_p, shift_p, wkv_p = _trunk(x_prompt, c_prompt, shift0, conv0, None, W)
    y_s, conv_s, shift_s, wkv_s = _trunk(x_sample, c_sample, state_shift, state_conv, state_wkv, W)
    return (y_p, y_s, conv_p, shift_p, wkv_p.astype(state_wkv.dtype), conv_s, shift_s, wkv_s.astype(state_wkv.dtype))
```

```python
import functools

import jax
import jax.numpy as jnp
from jax import lax
from jax.experimental import pallas as pl
from jax.experimental.pallas import tpu as pltpu

F32 = jnp.float32
BF16 = jnp.bfloat16

HEAD_DIM = 64
CONV_WIDTH = 31
CONV_BUF = CONV_WIDTH - 1
LORA_W = 64
LORA_A = 64
N_GROUPS = 4
EXPERTS_PER_GROUP = 8
N_EXPERTS = N_GROUPS * EXPERTS_PER_GROUP
RMS_EPS = 1e-6
LN_EPS = 1e-5
GN_EPS = 64e-5

V7X_LANES = 128
V7X_SUBLANES = 8
V7X_VMEM_LIMIT_BYTES = 56 * 1024 * 1024

ROW_TILE = 256
SHORT_SEQ_ROWS = 128
HIST_ROWS = 32
HIST_PAD = HIST_ROWS - CONV_BUF
CONV_LANE_CHUNK = 256
WKV_CHUNK = 64
WKV_GROUP_HEADS = 4
WKV_GROUP = WKV_GROUP_HEADS * HEAD_DIM
WKV_INTERLEAVE_PER_MATMUL = 2
ROUTER_LANES = V7X_LANES
ROUTE_E1, ROUTE_E2, ROUTE_W1, ROUTE_W2, ROUTE_R1, ROUTE_R2 = 0, 1, 2, 3, 4, 5
MOE_ROW_TILE = 512
MOE_ROW_TILE_SMALL = 128
MOE_SCATTER_TOKENS = 256
MOE_COMBINE_TOKENS = 128


def _dot(a, b):
    return jnp.dot(a, b, preferred_element_type=F32)


def _sigmoid(x):
    return 1.0 / (1.0 + jnp.exp(-x))


def _rmsnorm(x, g):
    return x * lax.rsqrt(jnp.mean(x * x, axis=-1, keepdims=True) + RMS_EPS) * g


def _params(sem):
    return pltpu.CompilerParams(dimension_semantics=sem, vmem_limit_bytes=V7X_VMEM_LIMIT_BYTES)


def _row_tiling(bsz, t):
    tm = min(t, ROW_TILE)
    bb = max(1, min(bsz, SHORT_SEQ_ROWS // tm))
    assert t % tm == 0 and bsz % bb == 0 and tm % V7X_SUBLANES == 0
    assert bb == 1 or tm == t
    return bb, tm


def _per_seq(ref, bb, n, d):
    p = ref[...]
    if bb == 1:
        return p[0]
    return jnp.broadcast_to(p, (bb, n, d)).reshape(bb * n, d)


def _full_spec(a):
    return pl.BlockSpec(a.shape, lambda b, i: (0,) * a.ndim)


def _moe_residual_operands(res, bb, tm):
    moe, gate = res
    d = moe.shape[-1]
    return ([moe, gate], [pl.BlockSpec((bb, tm, d), lambda b, i: (b, i, 0)),
                          pl.BlockSpec((bb, 1, d), lambda b, i: (b, 0, 0))])


def _ada_kernel(c_ref, w_ref, b_ref, o_ref):
    c = c_ref[...]
    s = c * _sigmoid(c)
    o_ref[0] = _dot(s.astype(BF16), w_ref[0].astype(BF16)) + b_ref[0]


def _modulation(c, ada_w, ada_b):
    depth, d, n = ada_w.shape
    bsz = c.shape[0]
    tn = 1536
    return pl.pallas_call(
        _ada_kernel,
        out_shape=jax.ShapeDtypeStruct((depth, bsz, n), F32),
        grid=(depth, n // tn),
        in_specs=[
            pl.BlockSpec((bsz, d), lambda l, j: (0, 0)),
            pl.BlockSpec((1, d, tn), lambda l, j: (l, 0, j)),
            pl.BlockSpec((1, 1, tn), lambda l, j: (l, 0, j)),
        ],
        out_specs=pl.BlockSpec((1, bsz, tn), lambda l, j: (l, 0, j)),
        compiler_params=_params(("arbitrary", "arbitrary")),
        name="ada_modulation",
    )(c, ada_w, ada_b.reshape(depth, 1, n))


def _mixer_pre_kernel(has_res, fuse_wkv, bb, tm, d, *refs):
    if has_res:
        xa_ref, moe_ref, gtp_ref, *refs = refs
    else:
        xa_ref, *refs = refs
    (sh_ref, sc_ref, g_ref, win_ref, shift0_ref, conv0_ref, mu_ref, cw_ref, cb_ref, clg_ref, clb_ref,
     w0_ref, lora_ref, a0_ref, gup_ref, *refs) = refs
    if fuse_wkv:
        *wkv_params, ones_ref, tri_ref = refs[:7]
        refs = refs[7:]
    if has_res:
        xres_ref, *refs = refs
    if fuse_wkv:
        (yag_ref, gb_ref, y_ref, convn_ref, shiftn_ref, sout_ref,
         ubuf_ref, z_ref, carry_ref, shifted_ref, s_ref) = refs
    else:
        (yag_ref, gb_ref, r_ref, lw_ref, k_ref, v_ref, a_ref, convn_ref, shiftn_ref,
         ubuf_ref, z_ref, carry_ref, shifted_ref) = refs
    c_plain = 4 * d
    rows = bb * tm
    t = pl.program_id(1)
    nt = pl.num_programs(1)

    @pl.when(t == 0)
    def _():
        hp = _per_seq(shift0_ref, bb, V7X_SUBLANES, d)
        hp = jnp.broadcast_to(hp, (bb * V7X_SUBLANES, d)).astype(BF16)
        carry_ref[...] = _dot(hp, win_ref[:, c_plain:])
        ubuf_ref[:, pl.ds(0, HIST_PAD), :] = jnp.zeros((bb, HIST_PAD, d), F32)
        ubuf_ref[:, pl.ds(HIST_PAD, CONV_BUF), :] = conv0_ref[...]
        if fuse_wkv:
            s_ref[...] = jnp.zeros_like(s_ref)

    x = xa_ref[...].reshape(rows, d)
    if has_res:
        x = x + _per_seq(gtp_ref, bb, tm, d) * moe_ref[...].reshape(rows, d)
        xres_ref[...] = x.reshape(bb, tm, d)
    h = _rmsnorm(x, g_ref[...]) * (1.0 + _per_seq(sc_ref, bb, tm, d)) + _per_seq(sh_ref, bb, tm, d)
    shiftn_ref[...] = h.reshape(bb, tm, d)[:, tm - 1:tm, :]
    hb = h.astype(BF16)
    cur = _dot(hb, win_ref[:, c_plain:])
    proj = _dot(hb, win_ref[:, :c_plain])

    prev = pltpu.roll(cur, 1, 0)
    row = lax.broadcasted_iota(jnp.int32, cur.shape, 0)
    if bb == 1:
        prev = jnp.where(row == 0, carry_ref[pl.ds(0, 1), :], prev)
        carry_ref[pl.ds(0, 1), :] = cur[tm - 1:tm, :]
    else:
        prev = jnp.where((row & (tm - 1)) == 0, carry_ref[...], prev)
    rw = cur + (prev - cur) * mu_ref[...]

    r = rw[:, 0:d]
    k = rw[:, d:2 * d]
    v = rw[:, 2 * d:3 * d]
    xwa = rw[:, 3 * d:3 * d + LORA_W + LORA_A]
    xg = rw[:, 3 * d + LORA_W + LORA_A:]
    lane = lax.broadcasted_iota(jnp.int32, xwa.shape, 1)
    lhs = jnp.where(lane < LORA_W, jnp.tanh(xwa), xwa)
    za = _dot(lhs.astype(BF16), lora_ref[...])
    log_decay = -jnp.exp(-0.5) * _sigmoid(w0_ref[...] + za[:, 0:d])
    a = _sigmoid(a0_ref[...] + za[:, d:])
    gate = _dot(_sigmoid(xg).astype(BF16), gup_ref[...])
    blk = (bb, tm, d)

    u = proj[:, 0:d] * _sigmoid(proj[:, d:2 * d])
    ubuf_ref[:, pl.ds(HIST_ROWS, tm), :] = u.reshape(blk)
    rb = min(tm, 32)
    lc = CONV_LANE_CHUNK
    span = tm + HIST_ROWS - V7X_SUBLANES

    def conv_shift(b, c0):
        for s in range(1, V7X_SUBLANES):
            shifted_ref[s - 1] = ubuf_ref[b, pl.ds(s, span), c0:c0 + lc]

    def conv_rows(b, c0, r0):
        acc = jnp.broadcast_to(cb_ref[:, c0:c0 + lc], (rb, lc))
        for tap in range(CONV_WIDTH):
            s = (HIST_PAD + tap) % V7X_SUBLANES
            base = r0 + HIST_PAD + tap - s
            if s == 0:
                src = ubuf_ref[b, pl.ds(base, rb), c0:c0 + lc]
            else:
                src = shifted_ref[s - 1, pl.ds(base, rb), :]
            acc = acc + cw_ref[pl.ds(tap, 1), c0:c0 + lc] * src
        z_ref[pl.ds(b * tm + r0, rb), c0:c0 + lc] = acc

    conv_work = []
    for c0 in range(0, d, lc):
        for b in range(bb):
            conv_work.append(functools.partial(conv_shift, b, c0))
            conv_work += [functools.partial(conv_rows, b, c0, r0) for r0 in range(0, tm, rb)]

    if fuse_wkv:
        y_ref[...] = _wkv_chunk_math(tm, d, r, log_decay, k, v, a, wkv_params, ones_ref, tri_ref, s_ref,
                                     conv_work).reshape(blk)
    else:
        r_ref[...] = r.reshape(blk)
        k_ref[...] = k.reshape(blk)
        v_ref[...] = v.reshape(blk)
        lw_ref[...] = log_decay.reshape(blk)
        a_ref[...] = a.reshape(blk)
    gb_ref[...] = (_sigmoid(proj[:, 3 * d:4 * d]) * gate).reshape(blk)

    for work in conv_work:
        work()
    z = z_ref[...]
    mu = jnp.mean(z, axis=-1, keepdims=True)
    zc = z - mu
    var = jnp.mean(zc * zc, axis=-1, keepdims=True)
    zn = zc * lax.rsqrt(var + LN_EPS) * clg_ref[...] + clb_ref[...]
    yag_ref[...] = (_sigmoid(proj[:, 2 * d:3 * d]) * (zn * _sigmoid(zn))).reshape(blk)

    convn_ref[...] = ubuf_ref[:, pl.ds(tm + HIST_PAD, CONV_BUF), :]
    if tm >= HIST_ROWS:
        ubuf_ref[:, pl.ds(0, HIST_ROWS), :] = ubuf_ref[:, pl.ds(tm, HIST_ROWS), :]

    if fuse_wkv:
        @pl.when(t == nt - 1)
        def _():
            for hd in range(d // HEAD_DIM):
                g, hs = divmod(hd, WKV_GROUP_HEADS)
                hs = slice(hs * HEAD_DIM, (hs + 1) * HEAD_DIM)
                sout_ref[0, hd] = s_ref[g, hs, hs]


def _mixer_pre(xa, res, sh, sc, g, win_bf, layer, shift0, conv0, mu, cw, cb, clg, clb, w0, lora_bf, a0, gup_bf,
               wkv_params=None):
    bsz, t, d = xa.shape
    bb, tm = _row_tiling(bsz, t)
    assert t == tm or tm >= HIST_ROWS
    assert bb == 1 or tm == V7X_SUBLANES
    fuse_wkv = wkv_params is not None
    c_in = win_bf.shape[2]
    c_rwkv = c_in - 4 * d
    tile = pl.BlockSpec((bb, tm, d), lambda b, i: (b, i, 0))
    perb = pl.BlockSpec((bb, 1, d), lambda b, i: (b, 0, 0))
    hist = pl.BlockSpec((bb, CONV_BUF, d), lambda b, i: (b, 0, 0))
    ins, specs = [xa], [tile]
    n_act = 3 if fuse_wkv else 7
    if res is not None:
        res_ins, res_specs = _moe_residual_operands(res, bb, tm)
        ins += res_ins
        specs += res_specs
        n_act += 1
    ins += [sh, sc, g, win_bf, shift0, conv0, mu, cw, cb, clg, clb, w0, lora_bf, a0, gup_bf]
    specs += [perb, perb, _full_spec(g),
              pl.BlockSpec((None, d, c_in), lambda b, i: (layer, 0, 0), pipeline_mode=pl.Buffered(1)),
              perb, hist] + [_full_spec(p) for p in (mu, cw, cb, clg, clb, w0, lora_bf, a0, gup_bf)]
    act = jax.ShapeDtypeStruct((bsz, t, d), F32)
    out_shape = [act] * n_act + [jax.ShapeDtypeStruct((bsz, CONV_BUF, d), F32), jax.ShapeDtypeStruct((bsz, 1, d), F32)]
    out_specs = [tile] * n_act + [hist, perb]
    scratch = [pltpu.VMEM((bb, tm + HIST_ROWS, d), F32), pltpu.VMEM((bb * tm, d), F32),
               pltpu.VMEM((bb * V7X_SUBLANES, c_rwkv), F32),
               pltpu.VMEM((V7X_SUBLANES - 1, tm + HIST_ROWS - V7X_SUBLANES, CONV_LANE_CHUNK), F32)]
    if fuse_wkv:
        assert bb == 1 and tm % WKV_CHUNK == 0 and d % WKV_GROUP == 0
        n_heads = d // HEAD_DIM
        consts = _wkv_constants()
        ins += list(wkv_params) + list(consts)
        specs += [_full_spec(p) for p in wkv_params] + [_full_spec(c) for c in consts]
        out_shape.append(jax.ShapeDtypeStruct((bsz, n_heads, HEAD_DIM, HEAD_DIM), F32))
        out_specs.append(pl.BlockSpec((1, n_heads, HEAD_DIM, HEAD_DIM), lambda b, i: (b, 0, 0, 0)))
        scratch.append(pltpu.VMEM((d // WKV_GROUP, WKV_GROUP, WKV_GROUP), F32))
    return pl.pallas_call(
        functools.partial(_mixer_pre_kernel, res is not None, fuse_wkv, bb, tm, d),
        out_shape=out_shape,
        grid=(bsz // bb, t // tm),
        in_specs=specs,
        out_specs=out_specs,
        scratch_shapes=scratch,
        compiler_params=_params(("arbitrary", "arbitrary")),
        name="mixer_pre",
    )(*ins)


def _wkv_kernel(tc, r_ref, lw_ref, k_ref, v_ref, a_ref, kkp_ref, kap_ref, rkp_ref, lng_ref, lnb_ref, s0_ref,
                y_ref, s_ref, o_ref):
    n = HEAD_DIM

    @pl.when(pl.program_id(1) == 0)
    def _():
        s_ref[...] = s0_ref[...]

    def step(tt, carry):
        r_t = r_ref[tt]
        k_t = k_ref[tt]
        a_t = a_ref[tt]
        w_t = jnp.exp(lw_ref[tt])
        kk = k_t * kkp_ref[...]
        kk = kk * lax.rsqrt(jnp.sum(kk * kk, axis=0, keepdims=True) + 1e-12)
        b_t = kk * a_t
        k2 = k_t * (1.0 + (a_t - 1.0) * kap_ref[...])

        def row(i, c):
            si = s_ref[i]
            sa = jnp.sum(si * kk, axis=0, keepdims=True)
            vi = v_ref[tt, pl.ds(i, 1), :]
            sn = si * w_t - sa * b_t + vi * k2
            s_ref[i] = sn
            o_ref[pl.ds(i, 1), :] = jnp.sum(sn * r_t, axis=0, keepdims=True)
            return c

        lax.fori_loop(0, n, row, 0, unroll=8)
        o = o_ref[...]
        mu = jnp.mean(o, axis=0, keepdims=True)
        oc = o - mu
        var = jnp.mean(oc * oc, axis=0, keepdims=True)
        yn = oc * lax.rsqrt(var + GN_EPS) * lng_ref[...] + lnb_ref[...]
        bonus = jnp.sum(r_t * k2 * rkp_ref[...], axis=0, keepdims=True) * v_ref[tt]
        y_ref[tt] = yn + bonus
        return carry

    lax.fori_loop(0, tc, step, 0)


def _wkv_scan(r, w, k, v, a, kkp, kap, rkp, lng, lnb, s0):
    t, n, nl = r.shape
    tc = min(t, 32)
    lb = V7X_LANES
    seq = pl.BlockSpec((tc, n, lb), lambda l, i: (i, 0, l))
    par = pl.BlockSpec((n, lb), lambda l, i: (0, 0))
    st = pl.BlockSpec((n, n, lb), lambda l, i: (0, 0, l))
    return pl.pallas_call(
        functools.partial(_wkv_kernel, tc),
        out_shape=[jax.ShapeDtypeStruct((t, n, nl), F32), jax.ShapeDtypeStruct((n, n, nl), F32)],
        grid=(nl // lb, t // tc),
        in_specs=[seq] * 5 + [par] * 5 + [st],
        out_specs=[seq, st],
        scratch_shapes=[pltpu.VMEM((n, lb), F32)],
        compiler_params=_params(("arbitrary", "arbitrary")),
        name="wkv_scan",
    )(r, w, k, v, a, kkp, kap, rkp, lng, lnb, s0)


def _bmm(a, b):
    return jnp.einsum("gck,gkn->gcn", a, b, preferred_element_type=F32)


def _bmm_nt(a, b):
    return jnp.einsum("gck,gnk->gcn", a, b, preferred_element_type=F32)


def _bmm_tn(a, b):
    return jnp.einsum("gtc,gtn->gcn", a, b, preferred_element_type=F32)


def _split_bf16(x):
    hi = x.astype(BF16)
    return hi, (x - hi.astype(F32)).astype(BF16)


def _wkv_constants():
    li = lax.broadcasted_iota(jnp.int32, (WKV_GROUP, WKV_GROUP), 0) // HEAD_DIM
    lj = lax.broadcasted_iota(jnp.int32, (WKV_GROUP, WKV_GROUP), 1) // HEAD_DIM
    ti = lax.broadcasted_iota(jnp.int32, (WKV_CHUNK, WKV_CHUNK), 0)
    tj = lax.broadcasted_iota(jnp.int32, (WKV_CHUNK, WKV_CHUNK), 1)
    return (li == lj).astype(BF16), (ti >= tj).astype(BF16)


def _wkv_chunk_math(tt, d, r, lw, k, v, a, params, ones_ref, tri_ref, s_ref, interleaved):
    c = WKV_CHUNK
    gw = WKV_GROUP
    ng = d // gw
    nc = tt // c
    n_items = nc * ng
    kkp_ref, kap_ref, rkp_ref, lng_ref, lnb_ref = params

    lane = lax.broadcasted_iota(jnp.int32, (1, c, gw), 2)
    rowi = lax.broadcasted_iota(jnp.int32, (1, c, gw), 1)
    head_of_lane = lax.shift_right_logical(lane, 6)
    hmask = [head_of_lane == h for h in range(WKV_GROUP_HEADS)]
    tcol = lane & (c - 1)
    strict = tcol < rowi
    incl = tcol <= rowi
    bl = lax.broadcasted_iota(jnp.int32, (1, gw, gw), 2)
    br = lax.broadcasted_iota(jnp.int32, (1, gw, gw), 1)
    blockmask = lax.shift_right_logical(bl, 6) == lax.shift_right_logical(br, 6)

    def items(x):
        x = x.reshape(nc, c, d)
        return jnp.stack([x[:, :, g * gw:(g + 1) * gw] for g in range(ng)], axis=1).reshape(n_items, c, gw)

    def per_group(ref):
        p = jnp.stack([ref[:, g * gw:(g + 1) * gw] for g in range(ng)], axis=0)
        return jnp.broadcast_to(p[None], (nc, ng, 1, gw)).reshape(n_items, 1, gw)

    def bd(xf):
        return jnp.concatenate([jnp.where(hmask[h], xf, 0.0) for h in range(WKV_GROUP_HEADS)], axis=1).astype(BF16)

    def segsum(x):
        shp = x.shape
        hi, lo = _split_bf16(x.reshape(-1, gw))
        return (_dot(hi, ones_ref[...]) + _dot(lo, ones_ref[...])).reshape(shp)

    def filled(y):
        for _ in range(WKV_INTERLEAVE_PER_MATMUL):
            if interleaved:
                interleaved.pop(0)()
        return y

    def bmm(x, y):
        return filled(_bmm(x, y))

    def bmm_nt(x, y):
        return filled(_bmm_nt(x, y))

    def bmm_tn(x, y):
        return filled(_bmm_tn(x, y))

    r, lw, k, v, a = (items(x) for x in (r, lw, k, v, a))
    kk = k * per_group(kkp_ref)
    kk = kk * lax.rsqrt(segsum(kk * kk) + 1e-12)
    b = kk * a
    k2 = k * (1.0 + (a - 1.0) * per_group(kap_ref))
    lw_hi, lw_lo = _split_bf16(lw)
    tri = jnp.broadcast_to(tri_ref[...][None], (n_items, c, c))
    cum = bmm(tri, lw_hi) + bmm(tri, lw_lo)
    cl = cum[:, c - 1:c, :]
    at = -(kk * jnp.exp(cum - lw))
    rt = r * jnp.exp(cum)
    winv = jnp.exp(-cum)
    wrem = jnp.exp(cl - cum)
    wc = jnp.exp(cl)
    lhs = jnp.concatenate([at, rt], axis=1).astype(BF16)
    rhs = jnp.concatenate([bd(b * winv), bd(k2 * winv)], axis=1)
    sc = bmm_nt(lhs, rhs)
    l_ab = jnp.where(strict, sc[:, 0:c, 0:gw], 0.0)
    l_ak = jnp.where(strict, sc[:, 0:c, gw:], 0.0)
    l_rb = jnp.where(incl, sc[:, c:, 0:gw], 0.0)
    l_rk = jnp.where(incl, sc[:, c:, gw:], 0.0)
    pk = l_ab
    tm = l_ab
    pbd = bd(pk)
    for _ in range(c.bit_length() - 2):
        pk = bmm(pk.astype(BF16), pbd)
        pbd = bd(pk)
        tm = tm + pk + bmm(tm.astype(BF16), pbd)
    tmb = tm.astype(BF16)
    vbd = bd(v)
    ah = at + bmm(tmb, bd(at))
    x1 = bmm(l_ak.astype(BF16), vbd)
    gm = x1 + bmm(tmb, bd(x1))
    l_rbb = l_rb.astype(BF16)
    rh = (rt + bmm(l_rbb, bd(ah))).astype(BF16)
    o_ind = bmm(jnp.concatenate([l_rbb, l_rk.astype(BF16)], axis=2), jnp.concatenate([bd(gm), vbd], axis=1))
    bb = (b * wrem).astype(BF16)
    kb = (k2 * wrem).astype(BF16)
    m = jnp.where(blockmask, bmm_tn(ah.astype(BF16), bb), 0.0).astype(BF16)
    n = jnp.where(blockmask, bmm_tn(jnp.concatenate([gm, v], axis=1).astype(BF16),
                                    jnp.concatenate([bb, kb], axis=1)), 0.0)

    s = s_ref[...]
    outs = []
    for ci in range(nc):
        sl = slice(ci * ng, (ci + 1) * ng)
        sb = s.astype(BF16)
        outs.append(bmm_nt(rh[sl], sb) + o_ind[sl])
        s = s * wc[sl] + bmm(sb, m[sl]) + n[sl]
    s_ref[...] = s

    inv_n = 1.0 / HEAD_DIM
    o = jnp.stack(outs, axis=0).reshape(n_items, c, gw)
    mu = segsum(o) * inv_n
    oc = o - mu
    var = segsum(oc * oc) * inv_n
    yn = oc * lax.rsqrt(var + GN_EPS) * per_group(lng_ref) + per_group(lnb_ref)
    bonus = segsum(r * k2 * per_group(rkp_ref)) * v
    y = (yn + bonus).reshape(nc, ng, c, gw)
    return jnp.concatenate([y[:, g].reshape(tt, gw) for g in range(ng)], axis=1)


def _mixer_post_kernel(bb, tm, d, x_ref, yag_ref, gb_ref, y_ref, gt1_ref, sh_ref, sc_ref, g_ref, wout_ref,
                       wrh_ref, wrl_ref, br_ref, tri_ref, x1_ref, h2_ref, route_ref, counts_ref, cnt_ref):
    rows = bb * tm

    @pl.when((pl.program_id(0) == 0) & (pl.program_id(1) == 0))
    def _():
        cnt_ref[...] = jnp.zeros_like(cnt_ref)

    x = x_ref[...].reshape(rows, d)
    merged = (yag_ref[...] + gb_ref[...] * y_ref[...]).reshape(rows, d)
    x1 = x + _per_seq(gt1_ref, bb, tm, d) * _dot(merged.astype(BF16), wout_ref[...])
    x1_ref[...] = x1.reshape(bb, tm, d)
    h2 = _rmsnorm(x1, g_ref[...]) * (1.0 + _per_seq(sc_ref, bb, tm, d)) + _per_seq(sh_ref, bb, tm, d)
    h2_ref[...] = h2.reshape(bb, tm, V7X_SUBLANES, d // V7X_SUBLANES)
    h_hi = h2.astype(BF16)
    h_lo = (h2 - h_hi.astype(F32)).astype(BF16)
    logits = _dot(h_hi, wrh_ref[...]) + (_dot(h_lo, wrh_ref[...]) + _dot(h_hi, wrl_ref[...])) + br_ref[...]

    lane_i = lax.broadcasted_iota(jnp.int32, logits.shape, 1)
    lane = lane_i.astype(F32)
    grp_of_lane = lax.shift_right_logical(lane_i, 3).astype(F32)
    neg = -jnp.inf
    none = 1e9
    is_g = (lane_i >= N_EXPERTS) & (lane_i < N_EXPERTS + N_GROUPS)
    glv = jnp.where(is_g, logits, neg)
    gmax = jnp.max(glv, axis=-1, keepdims=True)
    gidx = jnp.min(jnp.where(is_g & (glv == gmax), lane - N_EXPERTS, none), axis=-1, keepdims=True)
    pg = 1.0 / jnp.sum(jnp.where(is_g, jnp.exp(glv - gmax), 0.0), axis=-1, keepdims=True)
    in_grp = (lane_i < N_EXPERTS) & (grp_of_lane == gidx)
    ev = jnp.where(in_grp, logits, neg)
    v1 = jnp.max(ev, axis=-1, keepdims=True)
    i1 = jnp.min(jnp.where(in_grp & (ev == v1), lane, none), axis=-1, keepdims=True)
    ev2 = jnp.where(lane == i1, neg, ev)
    v2 = jnp.max(ev2, axis=-1, keepdims=True)
    i2 = jnp.min(jnp.where(in_grp & (ev2 == v2) & (lane != i1), lane, none), axis=-1, keepdims=True)
    e2 = jnp.exp(v2 - v1)
    den = 1.0 + e2
    hit = ((lane == i1) | (lane == i2)).astype(F32)
    before = cnt_ref[pl.ds(0, 1), :] + _dot(tri_ref[...], hit.astype(BF16))
    r1 = jnp.sum(jnp.where(lane == i1, before, 0.0), axis=-1, keepdims=True)
    r2 = jnp.sum(jnp.where(lane == i2, before, 0.0), axis=-1, keepdims=True)
    counts = cnt_ref[pl.ds(0, 1), :] + jnp.sum(hit, axis=0, keepdims=True)
    cnt_ref[pl.ds(0, 1), :] = counts
    counts_ref[...] = jnp.broadcast_to(counts, counts_ref.shape)
    route = (jnp.where(lane_i == ROUTE_E1, i1, 0.0) + jnp.where(lane_i == ROUTE_E2, i2, 0.0)
             + jnp.where(lane_i == ROUTE_W1, (1.0 / den) * pg, 0.0) + jnp.where(lane_i == ROUTE_W2, (e2 / den) * pg, 0.0)
             + jnp.where(lane_i == ROUTE_R1, r1, 0.0) + jnp.where(lane_i == ROUTE_R2, r2, 0.0))
    route_ref[...] = route.reshape(bb, tm, ROUTER_LANES)


def _mixer_post(x, yag, gb, y, gt1, sh, sc, g, wout_bf, wr_hi, wr_lo, br):
    bsz, t, d = x.shape
    bb, tm = _row_tiling(bsz, t)
    rows = bb * tm
    tile = pl.BlockSpec((bb, tm, d), lambda b, i: (b, i, 0))
    perb = pl.BlockSpec((bb, 1, d), lambda b, i: (b, 0, 0))
    ri = lax.broadcasted_iota(jnp.int32, (rows, rows), 0)
    ci = lax.broadcasted_iota(jnp.int32, (rows, rows), 1)
    tri = (ci < ri).astype(BF16)
    ins = [x, yag, gb, y, gt1, sh, sc, g, wout_bf, wr_hi, wr_lo, br, tri]
    specs = [tile, tile, tile, tile, perb, perb, perb] + [_full_spec(p) for p in (g, wout_bf, wr_hi, wr_lo, br, tri)]
    token_tiles = (bb, tm, V7X_SUBLANES, d // V7X_SUBLANES)
    return pl.pallas_call(
        functools.partial(_mixer_post_kernel, bb, tm, d),
        out_shape=[jax.ShapeDtypeStruct((bsz, t, d), F32),
                   jax.ShapeDtypeStruct((bsz, t, V7X_SUBLANES, d // V7X_SUBLANES), F32),
                   jax.ShapeDtypeStruct((bsz, t, ROUTER_LANES), F32),
                   jax.ShapeDtypeStruct((V7X_SUBLANES, ROUTER_LANES), F32)],
        grid=(bsz // bb, t // tm),
        in_specs=specs,
        out_specs=[tile, pl.BlockSpec(token_tiles, lambda b, i: (b, i, 0, 0)),
                   pl.BlockSpec((bb, tm, ROUTER_LANES), lambda b, i: (b, i, 0)),
                   pl.BlockSpec((V7X_SUBLANES, ROUTER_LANES), lambda b, i: (0, 0))],
        scratch_shapes=[pltpu.VMEM((V7X_SUBLANES, ROUTER_LANES), F32)],
        compiler_params=_params(("arbitrary", "arbitrary")),
        name="mixer_post",
    )(*ins)


def _moe_tables(route, counts, tm):
    n = route.shape[0]
    nt = 2 * n // tm
    cnt = counts[:N_EXPERTS].astype(jnp.int32)
    ends = jnp.cumsum(cnt)
    starts = ends - cnt
    ids = jnp.arange(N_EXPERTS, dtype=jnp.int32)[None, :]

    def position(e_lane, r_lane):
        e = route[:, e_lane].astype(jnp.int32)
        return jnp.sum(jnp.where(e[:, None] == ids, starts[None, :], 0), axis=1) + route[:, r_lane].astype(jnp.int32)

    pos = jnp.concatenate([position(ROUTE_E1, ROUTE_R1), position(ROUTE_E2, ROUTE_R2)])
    bounds = jnp.sort(jnp.concatenate([jnp.arange(nt + 1, dtype=jnp.int32) * tm, ends]))
    lo_abs, hi_abs = bounds[:-1], bounds[1:]
    tile = jnp.minimum(lo_abs // tm, nt - 1)
    nonzero = hi_abs > lo_abs
    expert = jnp.minimum(jnp.sum((ends[None, :] <= lo_abs[:, None]).astype(jnp.int32), axis=1), N_EXPERTS - 1)
    expert = jnp.maximum(lax.cummax(jnp.where(nonzero, expert, -1)), 0)
    prev_tile = jnp.concatenate([jnp.full((1,), -1, jnp.int32), lax.cummax(jnp.where(nonzero, tile, -1))[:-1]])
    first = (nonzero & (tile > prev_tile)).astype(jnp.int32)
    return pos, tile, expert, lo_abs - tile * tm, hi_abs - tile * tm, first


def _moe_scatter_kernel(ts, n, pos_ref, h_ref, xs_hbm, sem):
    i = pl.program_id(0)

    def copies(start):
        def body(j, c):
            for s in range(2):
                cp = pltpu.make_async_copy(h_ref.at[j], xs_hbm.at[pos_ref[s * n + i * ts + j]], sem)
                if start:
                    cp.start(priority=s)
                else:
                    cp.wait()
            return c
        lax.fori_loop(0, ts, body, 0, unroll=8)

    copies(True)
    copies(False)


def _moe_scatter(h2t, pos):
    n = h2t.shape[0]
    ts = min(n, MOE_SCATTER_TOKENS)
    grid_spec = pltpu.PrefetchScalarGridSpec(
        num_scalar_prefetch=1, grid=(n // ts,),
        in_specs=[pl.BlockSpec((ts, V7X_SUBLANES, V7X_LANES), lambda i, p: (i, 0, 0))],
        out_specs=pl.BlockSpec(memory_space=pl.ANY),
        scratch_shapes=[pltpu.SemaphoreType.DMA(())])
    return pl.pallas_call(
        functools.partial(_moe_scatter_kernel, ts, n),
        out_shape=jax.ShapeDtypeStruct((2 * n, V7X_SUBLANES, V7X_LANES), F32),
        grid_spec=grid_spec, compiler_params=_params(("arbitrary",)), name="moe_scatter")(pos, h2t)


def _moe_experts_kernel(tm, d, tile_ref, exp_ref, lo_ref, hi_ref, first_ref, xs_ref, wg_ref, wu_ref, wd_ref, ys_ref):
    w = pl.program_id(0)
    lo = lo_ref[w]
    hi = hi_ref[w]

    @pl.when(hi > lo)
    def _():
        x = xs_ref[...].reshape(tm, d).astype(BF16)
        hg = _dot(x, wg_ref[0].astype(BF16))
        hu = _dot(x, wu_ref[0].astype(BF16))
        act = hg * _sigmoid(hg) * hu
        y = _dot(act.astype(BF16), wd_ref[0].astype(BF16))
        row = lax.broadcasted_iota(jnp.int32, (tm, 1), 0)
        y = jnp.where((row >= lo) & (row < hi), y, 0.0).reshape(tm, V7X_SUBLANES, V7X_LANES)

        @pl.when(first_ref[w] == 1)
        def _():
            ys_ref[...] = y

        @pl.when(first_ref[w] == 0)
        def _():
            ys_ref[...] += y


def _moe_experts(xs, tile, expert, lo, hi, first, wg, wu, wd, layer, tm):
    d = V7X_SUBLANES * V7X_LANES
    f = wg.shape[-1]
    row_tile = pl.BlockSpec((tm, V7X_SUBLANES, V7X_LANES), lambda w, t, e, *_: (t[w], 0, 0))
    grid_spec = pltpu.PrefetchScalarGridSpec(
        num_scalar_prefetch=5, grid=(tile.shape[0],),
        in_specs=[row_tile,
                  pl.BlockSpec((None, 1, d, f), lambda w, t, e, *_: (layer, e[w], 0, 0)),
                  pl.BlockSpec((None, 1, d, f), lambda w, t, e, *_: (layer, e[w], 0, 0)),
                  pl.BlockSpec((None, 1, f, d), lambda w, t, e, *_: (layer, e[w], 0, 0))],
        out_specs=row_tile)
    return pl.pallas_call(
        functools.partial(_moe_experts_kernel, tm, d),
        out_shape=jax.ShapeDtypeStruct(xs.shape, F32),
        grid_spec=grid_spec, compiler_params=_params(("arbitrary",)), name="moe_experts")(
            tile, expert, lo, hi, first, xs, wg, wu, wd)


def _moe_combine_kernel(tc, n, d, pos_ref, ys_hbm, route_ref, o_ref, gbuf, sem):
    i = pl.program_id(0)
    nt = pl.num_programs(0)
    slot = lax.rem(i, 2)

    def copies(blk, sl, start):
        def body(j, c):
            for s in range(2):
                cp = pltpu.make_async_copy(ys_hbm.at[pos_ref[s * n + blk * tc + j]], gbuf.at[sl, s, j], sem.at[sl])
                if start:
                    cp.start(priority=s)
                else:
                    cp.wait()
            return c
        lax.fori_loop(0, tc, body, 0, unroll=8)

    @pl.when(i == 0)
    def _():
        copies(0, 0, True)

    @pl.when(i + 1 < nt)
    def _():
        copies(i + 1, 1 - slot, True)

    copies(i, slot, False)
    route = route_ref[...]
    y0 = gbuf[slot, 0].reshape(tc, d)
    y1 = gbuf[slot, 1].reshape(tc, d)
    o_ref[...] = route[:, ROUTE_W1:ROUTE_W1 + 1] * y0 + route[:, ROUTE_W2:ROUTE_W2 + 1] * y1


def _moe_combine(ys, pos, route):
    n = route.shape[0]
    d = V7X_SUBLANES * V7X_LANES
    tc = min(n, MOE_COMBINE_TOKENS)
    grid_spec = pltpu.PrefetchScalarGridSpec(
        num_scalar_prefetch=1, grid=(n // tc,),
        in_specs=[pl.BlockSpec(memory_space=pl.ANY),
                  pl.BlockSpec((tc, ROUTER_LANES), lambda i, p: (i, 0))],
        out_specs=pl.BlockSpec((tc, d), lambda i, p: (i, 0)),
        scratch_shapes=[pltpu.VMEM((2, 2, tc, V7X_SUBLANES, V7X_LANES), F32), pltpu.SemaphoreType.DMA((2,))])
    return pl.pallas_call(
        functools.partial(_moe_combine_kernel, tc, n, d),
        out_shape=jax.ShapeDtypeStruct((n, d), F32),
        grid_spec=grid_spec, compiler_params=_params(("arbitrary",)), name="moe_combine")(pos, ys, route)


def _moe_routed(h2t, route, counts, wg, wu, wd, layer):
    n = h2t.shape[0]
    assert h2t.shape[1:] == (V7X_SUBLANES, V7X_LANES)
    tm = MOE_ROW_TILE if n >= 32 * MOE_ROW_TILE else MOE_ROW_TILE_SMALL
    pos, tile, expert, lo, hi, first = _moe_tables(route, counts, tm)
    xs = _moe_scatter(h2t, pos)
    ys = _moe_experts(xs, tile, expert, lo, hi, first, wg, wu, wd, layer, tm)
    return _moe_combine(ys, pos, route)


def _final_kernel(bb, tm, d, xa_ref, moe_ref, gtp_ref, g_ref, o_ref):
    rows = bb * tm
    x = xa_ref[...].reshape(rows, d) + _per_seq(gtp_ref, bb, tm, d) * moe_ref[...].reshape(rows, d)
    o_ref[...] = _rmsnorm(x, g_ref[...]).reshape(bb, tm, d)


def _final_norm(xa, res, g):
    bsz, t, d = xa.shape
    bb, tm = _row_tiling(bsz, t)
    tile = pl.BlockSpec((bb, tm, d), lambda b, i: (b, i, 0))
    res_ins, res_specs = _moe_residual_operands(res, bb, tm)
    return pl.pallas_call(
        functools.partial(_final_kernel, bb, tm, d),
        out_shape=jax.ShapeDtypeStruct((bsz, t, d), F32),
        grid=(bsz // bb, t // tm),
        in_specs=[tile] + res_specs + [_full_spec(g)],
        out_specs=tile,
        compiler_params=_params(("arbitrary", "arbitrary")),
        name="final_norm",
    )(xa, *res_ins, g)


def _to_lanes(z, n_heads):
    bsz, t, _ = z.shape
    return z.reshape(bsz, t, n_heads, HEAD_DIM).transpose(1, 3, 0, 2).reshape(t, HEAD_DIM, bsz * n_heads)


def _from_lanes(z, bsz, n_heads):
    t = z.shape[0]
    return z.reshape(t, HEAD_DIM, bsz, n_heads).transpose(2, 0, 3, 1).reshape(bsz, t, n_heads * HEAD_DIM)


def _head_param(p, n_heads):
    return jnp.tile(p.reshape(n_heads, HEAD_DIM).T, (1, V7X_LANES // n_heads))


def _trunk(x, c, shift0, conv0, wkv0, W):
    bsz, t, d = x.shape
    n_heads = d // HEAD_DIM
    depth = W["w_in_bf"].shape[0]
    mod = _modulation(c, W["ada_w"], W["ada_b"])
    bufs, shifts, states = [], [], []
    res = None
    xa = x
    for l in range(depth):
        sh1, sc1, gt1, sh2, sc2, gt2 = (mod[l, :, i * d:(i + 1) * d][:, None, :] for i in range(6))
        row = lambda p: p[l][None, :]
        head_params = (W["rw_k_k"][l], W["rw_k_a"][l], W["rw_r_k"][l].reshape(-1), W["rw_lnx_g"][l], W["rw_lnx_b"][l])
        pre = _mixer_pre(
            xa, res, sh1, sc1, row(W["norm_mix_g"]), W["w_in_bf"], l, shift0[l][:, None, :], conv0[l],
            row(W["mu_shift"]), W["conv_w"][l], row(W["conv_b"]), row(W["conv_ln_g"]), row(W["conv_ln_b"]),
            row(W["rw_w0"]), W["lora_bf"][l], row(W["rw_a0"]), W["g_up_bf"][l],
            wkv_params=[p[None, :] for p in head_params] if wkv0 is None else None)
        if res is not None:
            xa, *pre = pre
        if wkv0 is None:
            yag, gb, y, convn, shiftn, s_l = pre
            states.append(s_l)
        else:
            yag, gb, r, lw, k, v, a, convn, shiftn = pre
            s0 = wkv0[l].transpose(2, 3, 0, 1).reshape(HEAD_DIM, HEAD_DIM, bsz * n_heads)
            y_l, s_l = _wkv_scan(*(_to_lanes(z, n_heads) for z in (r, lw, k, v, a)),
                                 *(_head_param(p, n_heads) for p in head_params), s0)
            y = _from_lanes(y_l, bsz, n_heads)
            states.append(s_l.reshape(HEAD_DIM, HEAD_DIM, bsz, n_heads).transpose(2, 3, 0, 1))
        x1, h2t, route, counts = _mixer_post(xa, yag, gb, y, gt1, sh2, sc2, row(W["norm_ffn_g"]), W["w_out_bf"][l],
                                             W["wr_hi"][l], W["wr_lo"][l], W["br"][l])
        moe = _moe_routed(h2t.reshape(bsz * t, V7X_SUBLANES, V7X_LANES), route.reshape(bsz * t, ROUTER_LANES),
                          counts[0], W["moe_w_gate"], W["moe_w_up"], W["moe_w_down"], l)
        xa, res = x1, (moe.reshape(bsz, t, d), gt2)
        bufs.append(convn)
        shifts.append(shiftn[:, 0, :])
    y_out = _final_norm(xa, res, W["final_norm_g"][None, :])
    return y_out, jnp.stack(bufs), jnp.stack(shifts), jnp.stack(states)


def kernel(x_prompt, x_sample, state_conv, state_shift, state_wkv, c_prompt, c_sample, norm_mix_g, norm_ffn_g, final_norm_g, ada_w, ada_b, w_in, mu_shift, conv_w, conv_b, conv_ln_g, conv_ln_b, rw_w0, rw_w_up, rw_a0, rw_a_up, rw_g_up, rw_k_k, rw_k_a, rw_r_k, rw_lnx_g, rw_lnx_b, w_out, moe_w_group, moe_b_group, moe_w_expert, moe_b_expert, moe_w_gate, moe_w_up, moe_w_down):
    depth, d, _ = w_in.shape
    zw = jnp.zeros_like(rw_w_up)
    lora = jnp.concatenate([jnp.concatenate([rw_w_up, zw], axis=2), jnp.concatenate([zw, rw_a_up], axis=2)], axis=1)
    pad = ROUTER_LANES - N_EXPERTS - N_GROUPS
    wr = jnp.concatenate([moe_w_expert, moe_w_group, jnp.zeros((depth, d, pad), F32)], axis=2)
    wr_hi = wr.astype(BF16)
    br = jnp.concatenate([moe_b_expert, moe_b_group, jnp.zeros((depth, pad), F32)], axis=1)[:, None, :]
    W = dict(norm_mix_g=norm_mix_g, norm_ffn_g=norm_ffn_g, final_norm_g=final_norm_g, ada_w=ada_w, ada_b=ada_b,
             w_in_bf=w_in.astype(BF16), mu_shift=mu_shift, conv_w=conv_w, conv_b=conv_b, conv_ln_g=conv_ln_g,
             conv_ln_b=conv_ln_b, rw_w0=rw_w0, lora_bf=lora.astype(BF16), rw_a0=rw_a0, g_up_bf=rw_g_up.astype(BF16),
             rw_k_k=rw_k_k, rw_k_a=rw_k_a, rw_r_k=rw_r_k, rw_lnx_g=rw_lnx_g, rw_lnx_b=rw_lnx_b,
             w_out_bf=w_out.astype(BF16), wr_hi=wr_hi, wr_lo=(wr - wr_hi.astype(F32)).astype(BF16), br=br,
             moe_w_gate=moe_w_gate, moe_w_up=moe_w_up, moe_w_down=moe_w_down)
    bp = x_prompt.shape[0]
    conv0 = jnp.zeros((depth, bp, CONV_BUF, d), x_prompt.dtype)
    shift0 = jnp.zeros((depth, bp, d), x_prompt.dtype)
    y_p, conv_p, shift_p, wkv_p = _trunk(x_prompt, c_prompt, shift0, conv0, None, W)
    y_s, conv_s, shift_s, wkv_s = _trunk(x_sample, c_sample, state_shift, state_conv, state_wkv, W)
    return (y_p, y_s, conv_p, shift_p, wkv_p.astype(state_wkv.dtype), conv_s, shift_s, wkv_s.astype(state_wkv.dtype))
```
